```python
import jax, jax.numpy as jnp
from jax import lax
import numpy as np

D_MODEL = 1024
BATCH = 8
SEQ = 2048
DEPTH = 4
DEC_BATCH = 128
DEC_SEQ = 4
PAST_LEN = 16384
PAGE_SIZE = 128

RET_HEADS = 4
RET_DK = 128
RET_DV = 128
D_RET = RET_HEADS * RET_DK
D_RET_V = RET_HEADS * RET_DV
RET_CHUNK = 128
CONV_CH = 512
CONV_W = 3
D_FF = 2816
PLE_DIM = 256
N_NORMS = 8
SPLITS = [D_RET, D_RET, D_RET_V, D_RET_V, CONV_CH, CONV_CH, CONV_CH, D_MODEL, D_MODEL]
N_IN = 2 * D_RET + 2 * D_RET_V + 3 * CONV_CH + 2 * D_MODEL
ROPE_BASE = 10000.0
EPS = 1e-6

kernel_name = "retention_shortconv_gated_hybrid_step"


def rmsnorm(x, g):
    xf = x.astype(jnp.float32)
    y = xf * lax.rsqrt(jnp.mean(xf * xf, axis=-1, keepdims=True) + EPS) * g.astype(jnp.float32)
    return y.astype(x.dtype)


def swiglu(x, wi, wo):
    gate, up = jnp.split(x @ wi, 2, axis=-1)
    return (jax.nn.silu(gate) * up) @ wo


def rotary(t, pos):
    d = t.shape[-1]
    inv_freq = ROPE_BASE ** (-jnp.arange(0, d, 2, dtype=jnp.float32) / d)
    ang = pos[:, None] * inv_freq[None, :]
    cos = jnp.cos(ang)[None, :, None, :].astype(t.dtype)
    sin = jnp.sin(ang)[None, :, None, :].astype(t.dtype)
    t1, t2 = jnp.split(t, 2, axis=-1)
    return jnp.concatenate([t1 * cos - t2 * sin, t1 * sin + t2 * cos], axis=-1)


def retention(q, k, v, s0):
    B, L, H, DK = q.shape
    DV = v.shape[-1]
    C = RET_CHUNK if L % RET_CHUNK == 0 else L
    N = L // C
    dt = q.dtype
    log_gamma = jnp.log1p(-jnp.exp2(-5.0 - jnp.arange(H, dtype=jnp.float32)))
    idx = jnp.arange(C, dtype=jnp.float32)
    diff = idx[:, None] - idx[None, :]
    dmask = jnp.where(diff[None] >= 0, jnp.exp(log_gamma[:, None, None] * jnp.maximum(diff, 0.0)[None]), 0.0).astype(dt)
    q_dec = jnp.exp(log_gamma[:, None] * (idx[None] + 1.0)).astype(dt)
    k_dec = jnp.exp(log_gamma[:, None] * (C - 1.0 - idx[None])).astype(dt)
    chunk_dec = jnp.exp(log_gamma * C).astype(dt)

    def to_chunks(t):
        return t.reshape(B, N, C, H, t.shape[-1]).transpose(1, 0, 3, 2, 4)

    def step(S, inp):
        qc, kc, vc = inp
        scores = jnp.einsum('bhcd,bhed->bhce', qc, kc) * dmask
        o = (jnp.einsum('bhce,bhev->bhcv', scores, vc)
             + jnp.einsum('bhcd,bhdv->bhcv', qc, S) * q_dec[..., None])
        S_new = chunk_dec[:, None, None] * S + jnp.einsum('bhcd,bhcv->bhdv', kc * k_dec[..., None], vc)
        return S_new.astype(S.dtype), o.astype(dt)

    S, o = lax.scan(step, s0, (to_chunks(q), to_chunks(k), to_chunks(v)))
    o = o.transpose(1, 0, 3, 2, 4).reshape(B, L, H, DV)
    return o, S


def mixer(xn, s_ret, s_conv, pos, w_in, ret_gn, w_ret_out, conv_w, w_conv_out, w_o):
    B, L, _ = xn.shape
    offs = np.cumsum(SPLITS)[:-1].tolist()
    q, k, v, g, cb, cc, ch, ga, gb = jnp.split(xn @ w_in, offs, axis=-1)
    q = rotary(q.reshape(B, L, RET_HEADS, RET_DK), pos)
    k = rotary(k.reshape(B, L, RET_HEADS, RET_DK), pos) * (RET_DK ** -0.5)
    v = v.reshape(B, L, RET_HEADS, RET_DV)
    o, s_ret_new = retention(q, k, v, s_ret)
    of = o.astype(jnp.float32)
    mu = jnp.mean(of, axis=-1, keepdims=True)
    var = jnp.mean((of - mu) ** 2, axis=-1, keepdims=True)
    on = (of - mu) * lax.rsqrt(var + EPS) * ret_gn.reshape(RET_HEADS, RET_DV).astype(jnp.float32)
    on = on.astype(xn.dtype).reshape(B, L, D_RET_V)
    u_a = (jax.nn.silu(g) * on) @ w_ret_out
    cin = cc * ch
    full = jnp.concatenate([s_conv.astype(cin.dtype), cin], axis=1)
    conv = full[:, 0:L] * conv_w[0]
    for j in range(1, CONV_W):
        conv = conv + full[:, j:j + L] * conv_w[j]
    s_conv_new = full[:, -(CONV_W - 1):]
    u_b = (cb * conv) @ w_conv_out
    merged = jax.nn.sigmoid(ga) * u_a + jax.nn.sigmoid(gb) * u_b
    return merged @ w_o, s_ret_new, s_conv_new


def trunk(x, p, s_ret, s_conv, pos, norm_g, w_ffn1_in, w_ffn1_out, w_in, ret_gn, w_ret_out,
          conv_w, w_conv_out, w_o, w_ffn2_in, w_ffn2_out, w_ple_gate, w_ple):
    h = x
    new_r = []
    new_c = []
    for i in range(DEPTH):
        g = norm_g[i]
        h = h + 0.5 * rmsnorm(swiglu(rmsnorm(h, g[0]), w_ffn1_in[i], w_ffn1_out[i]), g[1])
        mix, r, c = mixer(rmsnorm(h, g[2]), s_ret[i], s_conv[i], pos, w_in[i], ret_gn[i],
                          w_ret_out[i], conv_w[i], w_conv_out[i], w_o[i])
        h = h + rmsnorm(mix, g[3])
        h = h + 0.5 * rmsnorm(swiglu(rmsnorm(h, g[4]), w_ffn2_in[i], w_ffn2_out[i]), g[5])
        gate = jax.nn.sigmoid(rmsnorm(h, g[6]) @ w_ple_gate[i])
        h = h + rmsnorm((p[i] @ w_ple[i]) * gate, g[7])
        new_r.append(r)
        new_c.append(c)
    return h, jnp.stack(new_r), jnp.stack(new_c)


def setup_inputs(seed: int = 0) -> dict:
    key = jax.random.key(seed)
    ks = jax.random.split(key, 24)
    f32 = jnp.float32

    def nrm(k, shape, scale):
        return jax.random.normal(k, shape, f32) * scale

    return {
        "x_prompt": nrm(ks[0], (BATCH, SEQ, D_MODEL), 1.0),
        "x_sample": nrm(ks[1], (DEC_BATCH, DEC_SEQ, D_MODEL), 1.0),
        "state_ret": nrm(ks[2], (DEPTH, DEC_BATCH, RET_HEADS, RET_DK, RET_DV), 0.5),
        "state_conv": nrm(ks[3], (DEPTH, DEC_BATCH, CONV_W - 1, CONV_CH), 1.0),
        "p_prompt": nrm(ks[4], (DEPTH, BATCH, SEQ, PLE_DIM), 1.0),
        "p_sample": nrm(ks[5], (DEPTH, DEC_BATCH, DEC_SEQ, PLE_DIM), 1.0),
        "norm_g": 1.0 + nrm(ks[6], (DEPTH, N_NORMS, D_MODEL), 0.02),
        "w_ffn1_in": nrm(ks[7], (DEPTH, D_MODEL, 2 * D_FF), D_MODEL ** -0.5),
        "w_ffn1_out": nrm(ks[8], (DEPTH, D_FF, D_MODEL), D_FF ** -0.5),
        "w_in": nrm(ks[9], (DEPTH, D_MODEL, N_IN), D_MODEL ** -0.5),
        "ret_gn": 1.0 + nrm(ks[10], (DEPTH, D_RET_V), 0.02),
        "w_ret_out": nrm(ks[11], (DEPTH, D_RET_V, D_MODEL), D_RET_V ** -0.5),
        "conv_w": nrm(ks[12], (DEPTH, CONV_W, CONV_CH), CONV_W ** -0.5),
        "w_conv_out": nrm(ks[13], (DEPTH, CONV_CH, D_MODEL), CONV_CH ** -0.5),
        "w_o": nrm(ks[14], (DEPTH, D_MODEL, D_MODEL), D_MODEL ** -0.5),
        "w_ffn2_in": nrm(ks[15], (DEPTH, D_MODEL, 2 * D_FF), D_MODEL ** -0.5),
        "w_ffn2_out": nrm(ks[16], (DEPTH, D_FF, D_MODEL), D_FF ** -0.5),
        "w_ple_gate": nrm(ks[17], (DEPTH, D_MODEL, D_MODEL), D_MODEL ** -0.5),
        "w_ple": nrm(ks[18], (DEPTH, PLE_DIM, D_MODEL), PLE_DIM ** -0.5),
    }


def reference(x_prompt, x_sample, state_ret, state_conv, p_prompt, p_sample, norm_g, w_ffn1_in,
              w_ffn1_out, w_in, ret_gn, w_ret_out, conv_w, w_conv_out, w_o, w_ffn2_in, w_ffn2_out,
              w_ple_gate, w_ple):
    weights = (norm_g, w_ffn1_in, w_ffn1_out, w_in, ret_gn, w_ret_out, conv_w, w_conv_out, w_o,
               w_ffn2_in, w_ffn2_out, w_ple_gate, w_ple)
    s_ret0 = jnp.zeros((DEPTH, BATCH, RET_HEADS, RET_DK, RET_DV), x_prompt.dtype)
    s_conv0 = jnp.zeros((DEPTH, BATCH, CONV_W - 1, CONV_CH), x_prompt.dtype)
    pos_p = jnp.arange(SEQ, dtype=jnp.float32)
    y_prompt, ret_p, conv_p = trunk(x_prompt, p_prompt, s_ret0, s_conv0, pos_p, *weights)
    pos_s = PAST_LEN + jnp.arange(DEC_SEQ, dtype=jnp.float32)
    y_sample, ret_s, conv_s = trunk(x_sample, p_sample, state_ret.astype(x_sample.dtype),
                                    state_conv, pos_s, *weights)
    return (y_prompt, y_sample, ret_p, conv_p, ret_s, conv_s)
```

```python
import functools

import jax
import jax.numpy as jnp
from jax import lax
from jax.experimental import pallas as pl
from jax.experimental.pallas import tpu as pltpu

F32 = jnp.float32
BF16 = jnp.bfloat16

D_MODEL = 1024
BATCH = 8
SEQ = 2048
DEPTH = 4
DEC_BATCH = 128
DEC_SEQ = 4
PAST_LEN = 16384
HEADS = 4
HEAD_DIM = 128
D_RET = HEADS * HEAD_DIM
CONV_CH = 512
D_FF = 2816
PLE_DIM = 256
ROPE_BASE = 10000.0
EPS = 1e-6
K_SCALE = HEAD_DIM ** -0.5

N_PROMPT = BATCH * SEQ
N_SAMPLE = DEC_BATCH * DEC_SEQ
N_TOK = N_PROMPT + N_SAMPLE

OFF_QK = 0
OFF_VG = 2 * D_RET
OFF_CONV = 4 * D_RET
OFF_GATE = 4 * D_RET + 3 * CONV_CH
N_IN = OFF_GATE + 2 * D_MODEL

TOKEN_TILE = 512
RET_CHUNK = 256
FF_CHUNK = 256
N_FF_CHUNKS = D_FF // FF_CHUNK
SAMPLE_GROUP = 8
SAMPLE_BLOCK = 128
VMEM_LIMIT = 56 * 1024 * 1024


def _dot(a, b):
    return jnp.dot(a, b, preferred_element_type=F32)


def _dot_nt(a, b):
    return lax.dot_general(a, b, (((1,), (1,)), ((), ())), preferred_element_type=F32)


def _dot_tn(a, b):
    return lax.dot_general(a, b, (((0,), (0,)), ((), ())), preferred_element_type=F32)


def _rms(x, g):
    ms = jnp.mean(x * x, axis=-1, keepdims=True)
    return x * lax.rsqrt(ms + EPS) * g


def _silu(x):
    return x * jax.nn.sigmoid(x)


def _rotary(t, cos, sin_signed):
    return t * cos + pltpu.roll(t, HEAD_DIM // 2, 1) * sin_signed


def _group_norm(o, gn):
    mu = jnp.mean(o, axis=-1, keepdims=True)
    d = o - mu
    var = jnp.mean(d * d, axis=-1, keepdims=True)
    return d * lax.rsqrt(var + EPS) * gn


def _whole(shape):
    zeros = (0,) * len(shape)
    return pl.BlockSpec(shape, lambda *_: zeros, pipeline_mode=pl.Buffered(1))


def _ffn_kernel(x_ref, gpre_ref, gpost_ref, wi_ref, wo_ref, o_ref, act_ref):
    x = x_ref[...]
    xn = _rms(x, gpre_ref[...]).astype(BF16)
    for c in range(N_FF_CHUNKS):
        gu = _dot(xn, wi_ref[:, 2 * c * FF_CHUNK:2 * (c + 1) * FF_CHUNK])
        act = _silu(gu[:, :FF_CHUNK]) * gu[:, FF_CHUNK:]
        act_ref[:, c * FF_CHUNK:(c + 1) * FF_CHUNK] = act.astype(BF16)
    y = _dot(act_ref[...], wo_ref[...])
    o_ref[...] = x + 0.5 * _rms(y, gpost_ref[...])


def _ffn(h, gpre, gpost, wi, wo):
    row = pl.BlockSpec((TOKEN_TILE, D_MODEL), lambda i: (i, 0))
    return pl.pallas_call(
        _ffn_kernel,
        grid=(N_TOK // TOKEN_TILE,),
        in_specs=[row, _whole((1, D_MODEL)), _whole((1, D_MODEL)),
                  _whole((D_MODEL, 2 * D_FF)), _whole((D_FF, D_MODEL))],
        out_specs=row,
        out_shape=jax.ShapeDtypeStruct((N_TOK, D_MODEL), F32),
        scratch_shapes=[pltpu.VMEM((TOKEN_TILE, D_FF), BF16)],
        compiler_params=pltpu.CompilerParams(
            dimension_semantics=("arbitrary",), vmem_limit_bytes=VMEM_LIMIT),
        name="ffn",
    )(h, gpre, gpost, wi, wo)


def _ple_kernel(x_ref, p_ref, gpre_ref, gpost_ref, wg_ref, we_ref, o_ref):
    x = x_ref[...]
    xn = _rms(x, gpre_ref[...]).astype(BF16)
    gate = jax.nn.sigmoid(_dot(xn, wg_ref[...]))
    emb = _dot(p_ref[...].astype(BF16), we_ref[...])
    o_ref[...] = x + _rms(emb * gate, gpost_ref[...])


def _ple(h, p, gpre, gpost, wg, we):
    row = pl.BlockSpec((TOKEN_TILE, D_MODEL), lambda i: (i, 0))
    return pl.pallas_call(
        _ple_kernel,
        grid=(N_TOK // TOKEN_TILE,),
        in_specs=[row, pl.BlockSpec((TOKEN_TILE, PLE_DIM), lambda i: (i, 0)),
                  _whole((1, D_MODEL)), _whole((1, D_MODEL)),
                  _whole((D_MODEL, D_MODEL)), _whole((PLE_DIM, D_MODEL))],
        out_specs=row,
        out_shape=jax.ShapeDtypeStruct((N_TOK, D_MODEL), F32),
        compiler_params=pltpu.CompilerParams(
            dimension_semantics=("arbitrary",), vmem_limit_bytes=VMEM_LIMIT),
        name="ple",
    )(h, p, gpre, gpost, wg, we)


def _short_conv(cin, cw_ref, prev2, prev1, rows):
    sh1 = jnp.where(rows < 1, prev1, pltpu.roll(cin, 1, 0))
    sh2 = jnp.where(rows < 2, prev2, pltpu.roll(cin, 2, 0))
    return sh2 * cw_ref[0:1, :] + sh1 * cw_ref[1:2, :] + cin * cw_ref[2:3, :]


def _merge_out(x, ua, ub, gates, wo_ref, g3_ref):
    merged = jax.nn.sigmoid(gates[:, :D_MODEL]) * ua + jax.nn.sigmoid(gates[:, D_MODEL:]) * ub
    mix = _dot(merged.astype(BF16), wo_ref[...])
    return x + _rms(mix, g3_ref[...])


def _mixp_kernel(x_ref, g2_ref, g3_ref, win_ref, cos_ref, sin_ref, dmask_ref, qdec_ref,
                 kdec_ref, cdec_ref, gn_ref, wret_ref, cw_ref, wconv_ref, wo_ref,
                 o_ref, sret_ref, sconv_ref, ya_ref):
    t = pl.program_id(1)

    @pl.when(t == 0)
    def _():
        sret_ref[...] = jnp.zeros_like(sret_ref)
        sconv_ref[...] = jnp.zeros_like(sconv_ref)

    x = x_ref[...]
    xn = _rms(x, g2_ref[...]).astype(BF16)
    cos = cos_ref[...]
    sin = sin_ref[...]

    qk = _dot(xn, win_ref[:, OFF_QK:OFF_VG])
    vg = _dot(xn, win_ref[:, OFF_VG:OFF_CONV])
    for h in range(HEADS):
        hs = slice(h * HEAD_DIM, (h + 1) * HEAD_DIM)
        ks = slice(D_RET + h * HEAD_DIM, D_RET + (h + 1) * HEAD_DIM)
        qh = _rotary(qk[:, hs], cos, sin)
        kh = _rotary(qk[:, ks], cos, sin) * K_SCALE
        state = sret_ref[h]
        for c in range(TOKEN_TILE // RET_CHUNK):
            rs = slice(c * RET_CHUNK, (c + 1) * RET_CHUNK)
            qc = qh[rs].astype(BF16)
            kc = kh[rs]
            vc = vg[rs, hs].astype(BF16)
            scores = _dot_nt(qc, kc.astype(BF16)) * dmask_ref[h]
            o = _dot(scores.astype(BF16), vc) + _dot(qc, state.astype(BF16)) * qdec_ref[h]
            kd = (kc * kdec_ref[h]).astype(BF16)
            state = cdec_ref[h][0:1, :] * state + _dot_tn(kd, vc)
            on = _group_norm(o, gn_ref[:, hs])
            ya_ref[rs, hs] = (_silu(vg[rs, ks]) * on).astype(BF16)
        sret_ref[h] = state
    ua = _dot(ya_ref[...], wret_ref[...])

    cv = _dot(xn, win_ref[:, OFF_CONV:OFF_GATE])
    cin = cv[:, CONV_CH:2 * CONV_CH] * cv[:, 2 * CONV_CH:]
    rows = lax.broadcasted_iota(jnp.int32, (TOKEN_TILE, CONV_CH), 0)
    prev1 = sconv_ref[7:8, :]
    prev2 = jnp.where(rows < 1, sconv_ref[6:7, :], prev1)
    conv = _short_conv(cin, cw_ref, prev2, prev1, rows)
    sconv_ref[...] = cin[TOKEN_TILE - 8:, :]
    ub = _dot((cv[:, :CONV_CH] * conv).astype(BF16), wconv_ref[...])

    gates = _dot(xn, win_ref[:, OFF_GATE:])
    o_ref[...] = _merge_out(x, ua, ub, gates, wo_ref, g3_ref)


def _mix_prompt(h, g2, g3, win, cos, sin, dmask, qdec, kdec, cdec, gn, wret, cw, wconv, wo):
    nt = SEQ // TOKEN_TILE
    row = pl.BlockSpec((TOKEN_TILE, D_MODEL), lambda b, t: (b * nt + t, 0))
    pos = pl.BlockSpec((TOKEN_TILE, HEAD_DIM), lambda b, t: (t, 0))
    return pl.pallas_call(
        _mixp_kernel,
        grid=(BATCH, nt),
        in_specs=[row, _whole((1, D_MODEL)), _whole((1, D_MODEL)), _whole((D_MODEL, N_IN)),
                  pos, pos,
                  _whole((HEADS, RET_CHUNK, RET_CHUNK)), _whole((HEADS, RET_CHUNK, HEAD_DIM)),
                  _whole((HEADS, RET_CHUNK, HEAD_DIM)), _whole((HEADS, 8, HEAD_DIM)),
                  _whole((1, D_RET)), _whole((D_RET, D_MODEL)), _whole((8, CONV_CH)),
                  _whole((CONV_CH, D_MODEL)), _whole((D_MODEL, D_MODEL))],
        out_specs=[row,
                   pl.BlockSpec((None, HEADS, HEAD_DIM, HEAD_DIM), lambda b, t: (b, 0, 0, 0)),
                   pl.BlockSpec((None, 8, CONV_CH), lambda b, t: (b, 0, 0))],
        out_shape=[jax.ShapeDtypeStruct((N_TOK, D_MODEL), F32),
                   jax.ShapeDtypeStruct((BATCH, HEADS, HEAD_DIM, HEAD_DIM), F32),
                   jax.ShapeDtypeStruct((BATCH, 8, CONV_CH), F32)],
        scratch_shapes=[pltpu.VMEM((TOKEN_TILE, D_RET), BF16)],
        input_output_aliases={0: 0},
        compiler_params=pltpu.CompilerParams(
            dimension_semantics=("arbitrary", "arbitrary"), vmem_limit_bytes=VMEM_LIMIT),
        name="mix_prompt",
    )(h, g2, g3, win, cos, sin, dmask, qdec, kdec, cdec, gn, wret, cw, wconv, wo)


def _mixs_kernel(x_ref, g2_ref, g3_ref, win_ref, cos_ref, sin_ref, bdm_ref, qdec_ref,
                 kdec_ref, cdec_ref, gn_ref, wret_ref, cw_ref, wconv_ref, wo_ref,
                 scx_ref, st_ref,
                 o_ref, sret_ref, ctail_ref,
                 q_ref, kd_ref, v_ref, oa_ref, g_ref, ub_ref, gates_ref):
    i = pl.program_id(0)
    group_rows = SAMPLE_GROUP * DEC_SEQ

    @pl.when(i == 0)
    def _prologue():
        x = x_ref[...]
        xn = _rms(x, g2_ref[...]).astype(BF16)
        cos = cos_ref[...]
        sin = sin_ref[...]
        qk = _dot(xn, win_ref[:, OFF_QK:OFF_VG])
        vg = _dot(xn, win_ref[:, OFF_VG:OFF_CONV])
        v_ref[...] = vg[:, :D_RET]
        g_ref[...] = vg[:, D_RET:]
        for h in range(HEADS):
            hs = slice(h * HEAD_DIM, (h + 1) * HEAD_DIM)
            ks = slice(D_RET + h * HEAD_DIM, D_RET + (h + 1) * HEAD_DIM)
            qh = _rotary(qk[:, hs], cos, sin)
            kh = _rotary(qk[:, ks], cos, sin) * K_SCALE
            q_ref[:, hs] = qh
            for c in range(N_SAMPLE // SAMPLE_BLOCK):
                rs = slice(c * SAMPLE_BLOCK, (c + 1) * SAMPLE_BLOCK)
                kc = kh[rs]
                scores = _dot_nt(qh[rs].astype(BF16), kc.astype(BF16)) * bdm_ref[h]
                oa_ref[rs, hs] = _dot(scores.astype(BF16), vg[rs, hs].astype(BF16))
                kd_ref[rs, hs] = kc * kdec_ref[h]
        cv = _dot(xn, win_ref[:, OFF_CONV:OFF_GATE])
        cin = cv[:, CONV_CH:2 * CONV_CH] * cv[:, 2 * CONV_CH:]
        ctail_ref[...] = cin
        scx = scx_ref[...]
        tpos = lax.broadcasted_iota(jnp.int32, (N_SAMPLE, CONV_CH), 0) & (DEC_SEQ - 1)
        prev1 = pltpu.roll(scx, N_SAMPLE - 1, 0)
        conv = _short_conv(cin, cw_ref, scx, prev1, tpos)
        ub_ref[...] = _dot((cv[:, :CONV_CH] * conv).astype(BF16), wconv_ref[...])
        gates_ref[...] = _dot(xn, win_ref[:, OFF_GATE:])

    r0 = pl.multiple_of(i * group_rows, group_rows)
    first = lax.broadcasted_iota(jnp.int32, (8, HEAD_DIM), 0) < DEC_SEQ
    for h in range(HEADS):
        hs = slice(h * HEAD_DIM, (h + 1) * HEAD_DIM)
        cdec = cdec_ref[h][0:1, :]
        for j in range(SAMPLE_GROUP // 2):
            rt = pl.ds(pl.multiple_of(r0 + j * 8, 8), 8)
            qt = q_ref[rt, hs].astype(BF16)
            kt = kd_ref[rt, hs]
            vt = v_ref[rt, hs].astype(BF16)
            sa = st_ref[2 * j, h]
            sb = st_ref[2 * j + 1, h]
            cross = jnp.where(first, _dot(qt, sa.astype(BF16)), _dot(qt, sb.astype(BF16)))
            oa_ref[rt, hs] = oa_ref[rt, hs] + cross * qdec_ref[h]
            ka = jnp.where(first, kt, 0.0).astype(BF16)
            kb = jnp.where(first, 0.0, kt).astype(BF16)
            sret_ref[2 * j, h] = cdec * sa + _dot_tn(ka, vt)
            sret_ref[2 * j + 1, h] = cdec * sb + _dot_tn(kb, vt)

    @pl.when(i == pl.num_programs(0) - 1)
    def _epilogue():
        for h in range(HEADS):
            hs = slice(h * HEAD_DIM, (h + 1) * HEAD_DIM)
            on = _group_norm(oa_ref[:, hs], gn_ref[:, hs])
            g_ref[:, hs] = _silu(g_ref[:, hs]) * on
        ua = _dot(g_ref[...].astype(BF16), wret_ref[...])
        o_ref[...] = _merge_out(x_ref[...], ua, ub_ref[...], gates_ref[...], wo_ref, g3_ref)


def _mix_sample(h, g2, g3, win, cos, sin, bdm, qdec, kdec, cdec, gn, wret, cw, wconv, wo,
                scx, state_ret, layer):
    row = pl.BlockSpec((N_SAMPLE, D_MODEL), lambda i: (N_PROMPT // N_SAMPLE, 0))
    st_block = (SAMPLE_GROUP, HEADS, HEAD_DIM, HEAD_DIM)
    return pl.pallas_call(
        _mixs_kernel,
        grid=(DEC_BATCH // SAMPLE_GROUP,),
        in_specs=[row, _whole((1, D_MODEL)), _whole((1, D_MODEL)), _whole((D_MODEL, N_IN)),
                  _whole((N_SAMPLE, HEAD_DIM)), _whole((N_SAMPLE, HEAD_DIM)),
                  _whole((HEADS, SAMPLE_BLOCK, SAMPLE_BLOCK)), _whole((HEADS, 8, HEAD_DIM)),
                  _whole((HEADS, SAMPLE_BLOCK, HEAD_DIM)), _whole((HEADS, 8, HEAD_DIM)),
                  _whole((1, D_RET)), _whole((D_RET, D_MODEL)), _whole((8, CONV_CH)),
                  _whole((CONV_CH, D_MODEL)), _whole((D_MODEL, D_MODEL)),
                  _whole((N_SAMPLE, CONV_CH)),
                  pl.BlockSpec((None,) + st_block, lambda i: (layer, i, 0, 0, 0))],
        out_specs=[row,
                   pl.BlockSpec(st_block, lambda i: (i, 0, 0, 0)),
                   pl.BlockSpec((N_SAMPLE, CONV_CH), lambda i: (0, 0))],
        out_shape=[jax.ShapeDtypeStruct((N_TOK, D_MODEL), F32),
                   jax.ShapeDtypeStruct((DEC_BATCH, HEADS, HEAD_DIM, HEAD_DIM), F32),
                   jax.ShapeDtypeStruct((N_SAMPLE, CONV_CH), F32)],
        scratch_shapes=[pltpu.VMEM((N_SAMPLE, D_RET), F32),
                        pltpu.VMEM((N_SAMPLE, D_RET), F32),
                        pltpu.VMEM((N_SAMPLE, D_RET), F32),
                        pltpu.VMEM((N_SAMPLE, D_RET), F32),
                        pltpu.VMEM((N_SAMPLE, D_RET), F32),
                        pltpu.VMEM((N_SAMPLE, D_MODEL), F32),
                        pltpu.VMEM((N_SAMPLE, 2 * D_MODEL), F32)],
        input_output_aliases={0: 0},
        compiler_params=pltpu.CompilerParams(
            dimension_semantics=("arbitrary",), vmem_limit_bytes=VMEM_LIMIT),
        name="mix_sample",
    )(h, g2, g3, win, cos, sin, bdm, qdec, kdec, cdec, gn, wret, cw, wconv, wo, scx, state_ret)


def _rope_tables(pos):
    inv_freq = ROPE_BASE ** (-jnp.arange(0, HEAD_DIM, 2, dtype=F32) / HEAD_DIM)
    ang = pos[:, None] * inv_freq[None, :]
    cos = jnp.cos(ang)
    sin = jnp.sin(ang)
    return jnp.concatenate([cos, cos], -1), jnp.concatenate([-sin, sin], -1)


def _decay_tables(chunk):
    log_gamma = jnp.log1p(-jnp.exp2(-5.0 - jnp.arange(HEADS, dtype=F32)))
    idx = jnp.arange(chunk, dtype=F32)
    diff = idx[:, None] - idx[None, :]
    dmask = jnp.where(diff[None] >= 0,
                      jnp.exp(log_gamma[:, None, None] * jnp.maximum(diff, 0.0)[None]), 0.0)
    qdec = jnp.exp(log_gamma[:, None] * (idx[None] + 1.0))
    kdec = jnp.exp(log_gamma[:, None] * (chunk - 1.0 - idx[None]))
    cdec = jnp.exp(log_gamma * chunk)
    return dmask, qdec, kdec, cdec


def _lanes(t, reps):
    t = jnp.tile(t, (1, reps))
    return jnp.broadcast_to(t[:, :, None], t.shape + (HEAD_DIM,))


def _interleave_gate_up(wi):
    gate = wi[:, :D_FF].reshape(D_MODEL, N_FF_CHUNKS, FF_CHUNK)
    up = wi[:, D_FF:].reshape(D_MODEL, N_FF_CHUNKS, FF_CHUNK)
    return jnp.stack([gate, up], axis=2).reshape(D_MODEL, 2 * D_FF)


def kernel(x_prompt, x_sample, state_ret, state_conv, p_prompt, p_sample, norm_g, w_ffn1_in,
           w_ffn1_out, w_in, ret_gn, w_ret_out, conv_w, w_conv_out, w_o, w_ffn2_in, w_ffn2_out,
           w_ple_gate, w_ple):
    h = jnp.concatenate([x_prompt.reshape(N_PROMPT, D_MODEL),
                         x_sample.reshape(N_SAMPLE, D_MODEL)], axis=0)
    p_all = jnp.concatenate([p_prompt.reshape(DEPTH, N_PROMPT, PLE_DIM),
                             p_sample.reshape(DEPTH, N_SAMPLE, PLE_DIM)], axis=1)

    cos_p, sin_p = _rope_tables(jnp.arange(SEQ, dtype=F32))
    cos_s, sin_s = _rope_tables(PAST_LEN + jnp.arange(DEC_SEQ, dtype=F32))
    cos_s = jnp.tile(cos_s, (DEC_BATCH, 1))
    sin_s = jnp.tile(sin_s, (DEC_BATCH, 1))

    dmask_p, qdec_p, kdec_p, cdec_p = _decay_tables(RET_CHUNK)
    qdec_p = _lanes(qdec_p, 1)
    kdec_p = _lanes(kdec_p, 1)
    cdec_p = jnp.broadcast_to(cdec_p[:, None, None], (HEADS, 8, HEAD_DIM))

    dmask_s, qdec_s, kdec_s, cdec_s = _decay_tables(DEC_SEQ)
    seqs_per_block = SAMPLE_BLOCK // DEC_SEQ
    eye = jnp.eye(seqs_per_block, dtype=F32)
    bdm_s = jnp.einsum("ab,hij->haibj", eye, dmask_s).reshape(HEADS, SAMPLE_BLOCK, SAMPLE_BLOCK)
    qdec_s = _lanes(qdec_s, 8 // DEC_SEQ)
    kdec_s = _lanes(kdec_s, seqs_per_block)
    cdec_s = jnp.broadcast_to(cdec_s[:, None, None], (HEADS, 8, HEAD_DIM))

    ret_p, conv_p, ret_s, conv_s = [], [], [], []
    for i in range(DEPTH):
        g = norm_g[i][:, None, :]
        cw = jnp.concatenate([conv_w[i], jnp.zeros((8 - conv_w.shape[1], CONV_CH), F32)], 0)
        gn = ret_gn[i][None, :]
        win = w_in[i].astype(BF16)
        wret = w_ret_out[i].astype(BF16)
        wconv = w_conv_out[i].astype(BF16)
        wo = w_o[i].astype(BF16)
        scx = jnp.concatenate([state_conv[i], jnp.zeros_like(state_conv[i])], axis=1)
        scx = scx.reshape(N_SAMPLE, CONV_CH)

        h = _ffn(h, g[0], g[1], _interleave_gate_up(w_ffn1_in[i]).astype(BF16),
                 w_ffn1_out[i].astype(BF16))
        h, rp, cp = _mix_prompt(h, g[2], g[3], win, cos_p, sin_p, dmask_p, qdec_p, kdec_p, cdec_p,
                                gn, wret, cw, wconv, wo)
        h, rs, ct = _mix_sample(h, g[2], g[3], win, cos_s, sin_s, bdm_s, qdec_s, kdec_s, cdec_s,
                                gn, wret, cw, wconv, wo, scx, state_ret, i)
        h = _ffn(h, g[4], g[5], _interleave_gate_up(w_ffn2_in[i]).astype(BF16),
                 w_ffn2_out[i].astype(BF16))
        h = _ple(h, p_all[i], g[6], g[7], w_ple_gate[i].astype(BF16), w_ple[i].astype(BF16))
        ret_p.append(rp)
        conv_p.append(cp[:, 8 - 2:, :])
        ret_s.append(rs)
        conv_s.append(ct.reshape(DEC_BATCH, DEC_SEQ, CONV_CH)[:, DEC_SEQ - 2:, :])

    y_prompt = h[:N_PROMPT].reshape(BATCH, SEQ, D_MODEL)
    y_sample = h[N_PROMPT:].reshape(DEC_BATCH, DEC_SEQ, D_MODEL)
    return (y_prompt, y_sample, jnp.stack(ret_p), jnp.stack(conv_p),
            jnp.stack(ret_s), jnp.stack(conv_s))
```

```python
import jax
import jax.numpy as jnp
from jax import lax
from jax.experimental import pallas as pl
from jax.experimental.pallas import tpu as pltpu

F32 = jnp.float32
BF16 = jnp.bfloat16

D_MODEL = 1024
BATCH = 8
SEQ = 2048
DEPTH = 4
DEC_BATCH = 128
DEC_SEQ = 4
PAST_LEN = 16384
HEADS = 4
HEAD_DIM = 128
D_RET = HEADS * HEAD_DIM
CONV_CH = 512
D_FF = 2816
PLE_DIM = 256
ROPE_BASE = 10000.0
EPS = 1e-6
K_SCALE = HEAD_DIM ** -0.5

N_PROMPT = BATCH * SEQ
N_SAMPLE = DEC_BATCH * DEC_SEQ

OFF_QK = 0
OFF_VG = 2 * D_RET
OFF_CONV = 4 * D_RET
OFF_GATE = 4 * D_RET + 3 * CONV_CH
N_IN = OFF_GATE + 2 * D_MODEL

TOKEN_TILE = N_SAMPLE
N_PROMPT_TILES = N_PROMPT // TOKEN_TILE
RET_CHUNK = 256
FF_CHUNK = 256
N_FF_CHUNKS = D_FF // FF_CHUNK
SAMPLE_GROUP = 8
SAMPLE_BLOCK = 128
VMEM_LIMIT = 56 * 1024 * 1024


def _dot(a, b):
    return jnp.dot(a, b, preferred_element_type=F32)


def _dot_nt(a, b):
    return lax.dot_general(a, b, (((1,), (1,)), ((), ())), preferred_element_type=F32)


def _dot_tn(a, b):
    return lax.dot_general(a, b, (((0,), (0,)), ((), ())), preferred_element_type=F32)


def _rms(x, g):
    ms = jnp.mean(x * x, axis=-1, keepdims=True)
    return x * lax.rsqrt(ms + EPS) * g


def _silu(x):
    return x * jax.nn.sigmoid(x)


def _rotary(t, cos, sin_signed):
    return t * cos + pltpu.roll(t, HEAD_DIM // 2, 1) * sin_signed


def _group_norm(o, gn):
    mu = jnp.mean(o, axis=-1, keepdims=True)
    d = o - mu
    var = jnp.mean(d * d, axis=-1, keepdims=True)
    return d * lax.rsqrt(var + EPS) * gn


def _whole(shape):
    zeros = (0,) * len(shape)
    return pl.BlockSpec(shape, lambda *_: zeros, pipeline_mode=pl.Buffered(1))


def _prompt_rows(width, layer=None):
    if layer is None:
        return pl.BlockSpec((TOKEN_TILE, width),
                            lambda i: (jnp.minimum(i, N_PROMPT_TILES - 1), 0))
    return pl.BlockSpec((None, TOKEN_TILE, width),
                        lambda i: (layer, jnp.minimum(i, N_PROMPT_TILES - 1), 0))


def _sample_rows(width, layer=None):
    if layer is None:
        return pl.BlockSpec((N_SAMPLE, width), lambda i: (0, 0))
    return pl.BlockSpec((None, N_SAMPLE, width), lambda i: (layer, 0, 0))


def _load_rows(p_ref, s_ref):
    return jnp.where(pl.program_id(0) < N_PROMPT_TILES, p_ref[...], s_ref[...])


def _store_rows(p_ref, s_ref, val):
    i = pl.program_id(0)

    @pl.when(i < N_PROMPT_TILES)
    def _():
        p_ref[...] = val

    @pl.when(i == N_PROMPT_TILES)
    def _():
        s_ref[...] = val


def _ffn_body(x, gpre_ref, gpost_ref, wi_ref, wo_ref, act_ref):
    xn = _rms(x, gpre_ref[...]).astype(BF16)
    for c in range(N_FF_CHUNKS):
        cs = slice(c * FF_CHUNK, (c + 1) * FF_CHUNK)
        us = slice(D_FF + c * FF_CHUNK, D_FF + (c + 1) * FF_CHUNK)
        act = _silu(_dot(xn, wi_ref[:, cs])) * _dot(xn, wi_ref[:, us])
        act_ref[:, cs] = act.astype(BF16)
    y = _dot(act_ref[...], wo_ref[...])
    return x + 0.5 * _rms(y, gpost_ref[...])


def _ffn_kernel(xp_ref, xs_ref, gpre_ref, gpost_ref, wi_ref, wo_ref, op_ref, os_ref, act_ref):
    x = _load_rows(xp_ref, xs_ref)
    _store_rows(op_ref, os_ref, _ffn_body(x, gpre_ref, gpost_ref, wi_ref, wo_ref, act_ref))


def _ffn_ple_kernel(xp_ref, xs_ref, pp_ref, ps_ref, gpre_ref, gpost_ref, wi_ref, wo_ref,
                    g6_ref, g7_ref, wg_ref, we_ref, op_ref, os_ref, act_ref):
    x = _load_rows(xp_ref, xs_ref)
    x = _ffn_body(x, gpre_ref, gpost_ref, wi_ref, wo_ref, act_ref)
    xn = _rms(x, g6_ref[...]).astype(BF16)
    gate = jax.nn.sigmoid(_dot(xn, wg_ref[...]))
    emb = _dot(_load_rows(pp_ref, ps_ref).astype(BF16), we_ref[...])
    _store_rows(op_ref, os_ref, x + _rms(emb * gate, g7_ref[...]))


def _token_call(body, name, hp, hs, extra_specs, extra_args):
    return pl.pallas_call(
        body,
        grid=(N_PROMPT_TILES + 1,),
        in_specs=[_prompt_rows(D_MODEL), _sample_rows(D_MODEL)] + extra_specs,
        out_specs=[_prompt_rows(D_MODEL), _sample_rows(D_MODEL)],
        out_shape=[jax.ShapeDtypeStruct((N_PROMPT, D_MODEL), F32),
                   jax.ShapeDtypeStruct((N_SAMPLE, D_MODEL), F32)],
        scratch_shapes=[pltpu.VMEM((TOKEN_TILE, D_FF), BF16)],
        compiler_params=pltpu.CompilerParams(
            dimension_semantics=("arbitrary",), vmem_limit_bytes=VMEM_LIMIT),
        name=name,
    )(hp, hs, *extra_args)


def _ffn(hp, hs, gpre, gpost, wi, wo):
    gain = _whole((1, D_MODEL))
    specs = [gain, gain, _whole((D_MODEL, 2 * D_FF)), _whole((D_FF, D_MODEL))]
    return _token_call(_ffn_kernel, "ffn", hp, hs, specs, (gpre, gpost, wi, wo))


def _ffn_ple(hp, hs, pp, ps, layer, gpre, gpost, wi, wo, g6, g7, wg, we):
    gain = _whole((1, D_MODEL))
    specs = [_prompt_rows(PLE_DIM, layer), _sample_rows(PLE_DIM, layer), gain, gain,
             _whole((D_MODEL, 2 * D_FF)), _whole((D_FF, D_MODEL)), gain, gain,
             _whole((D_MODEL, D_MODEL)), _whole((PLE_DIM, D_MODEL))]
    return _token_call(_ffn_ple_kernel, "ffn_ple", hp, hs, specs,
                       (pp, ps, gpre, gpost, wi, wo, g6, g7, wg, we))


def _short_conv(cin, cw_ref, prev2, prev1, rows):
    sh1 = jnp.where(rows < 1, prev1, pltpu.roll(cin, 1, 0))
    sh2 = jnp.where(rows < 2, prev2, pltpu.roll(cin, 2, 0))
    return sh2 * cw_ref[0:1, :] + sh1 * cw_ref[1:2, :] + cin * cw_ref[2:3, :]


def _merge_out(x, ua, ub, gates, wo_ref, g3_ref):
    merged = jax.nn.sigmoid(gates[:, :D_MODEL]) * ua + jax.nn.sigmoid(gates[:, D_MODEL:]) * ub
    mix = _dot(merged.astype(BF16), wo_ref[...])
    return x + _rms(mix, g3_ref[...])


def _mixer_weight_specs():
    return [_whole((1, D_MODEL)), _whole((1, D_MODEL)), _whole((D_MODEL, N_IN)),
            _whole((1, D_RET)), _whole((D_RET, D_MODEL)), _whole((8, CONV_CH)),
            _whole((CONV_CH, D_MODEL)), _whole((D_MODEL, D_MODEL))]


def _mixp_kernel(x_ref, g2_ref, g3_ref, win_ref, gn_ref, wret_ref, cw_ref, wconv_ref, wo_ref,
                 cos_ref, sin_ref, dmask_ref, qdec_ref, kdec_ref, cdec_ref,
                 o_ref, sret_ref, sconv_ref, ya_ref):
    t = pl.program_id(1)

    @pl.when(t == 0)
    def _():
        sret_ref[...] = jnp.zeros_like(sret_ref)
        sconv_ref[...] = jnp.zeros_like(sconv_ref)

    x = x_ref[...]
    xn = _rms(x, g2_ref[...]).astype(BF16)
    cos = cos_ref[...]
    sin = sin_ref[...]

    qk = _dot(xn, win_ref[:, OFF_QK:OFF_VG])
    vg = _dot(xn, win_ref[:, OFF_VG:OFF_CONV])
    for h in range(HEADS):
        hs = slice(h * HEAD_DIM, (h + 1) * HEAD_DIM)
        ks = slice(D_RET + h * HEAD_DIM, D_RET + (h + 1) * HEAD_DIM)
        qh = _rotary(qk[:, hs], cos, sin)
        kh = _rotary(qk[:, ks], cos, sin) * K_SCALE
        state = sret_ref[h]
        for c in range(TOKEN_TILE // RET_CHUNK):
            rs = slice(c * RET_CHUNK, (c + 1) * RET_CHUNK)
            qc = qh[rs].astype(BF16)
            kc = kh[rs]
            vc = vg[rs, hs].astype(BF16)
            scores = _dot_nt(qc, kc.astype(BF16)) * dmask_ref[h]
            o = _dot(scores.astype(BF16), vc) + _dot(qc, state.astype(BF16)) * qdec_ref[h]
            kd = (kc * kdec_ref[h]).astype(BF16)
            state = cdec_ref[h][0:1, :] * state + _dot_tn(kd, vc)
            on = _group_norm(o, gn_ref[:, hs])
            ya_ref[rs, hs] = (_silu(vg[rs, ks]) * on).astype(BF16)
        sret_ref[h] = state
    ua = _dot(ya_ref[...], wret_ref[...])

    cv = _dot(xn, win_ref[:, OFF_CONV:OFF_GATE])
    cin = cv[:, CONV_CH:2 * CONV_CH] * cv[:, 2 * CONV_CH:]
    rows = lax.broadcasted_iota(jnp.int32, (TOKEN_TILE, CONV_CH), 0)
    prev1 = sconv_ref[7:8, :]
    prev2 = jnp.where(rows < 1, sconv_ref[6:7, :], prev1)
    conv = _short_conv(cin, cw_ref, prev2, prev1, rows)
    sconv_ref[...] = cin[TOKEN_TILE - 8:, :]
    ub = _dot((cv[:, :CONV_CH] * conv).astype(BF16), wconv_ref[...])

    gates = _dot(xn, win_ref[:, OFF_GATE:])
    o_ref[...] = _merge_out(x, ua, ub, gates, wo_ref, g3_ref)


def _mix_prompt(hp, weights, tables):
    nt = SEQ // TOKEN_TILE
    row = pl.BlockSpec((TOKEN_TILE, D_MODEL), lambda b, t: (b * nt + t, 0))
    pos = pl.BlockSpec((TOKEN_TILE, HEAD_DIM), lambda b, t: (t, 0))
    return pl.pallas_call(
        _mixp_kernel,
        grid=(BATCH, nt),
        in_specs=[row] + _mixer_weight_specs() + [
            pos, pos,
            _whole((HEADS, RET_CHUNK, RET_CHUNK)), _whole((HEADS, RET_CHUNK, HEAD_DIM)),
            _whole((HEADS, RET_CHUNK, HEAD_DIM)), _whole((HEADS, 8, HEAD_DIM))],
        out_specs=[row,
                   pl.BlockSpec((None, HEADS, HEAD_DIM, HEAD_DIM), lambda b, t: (b, 0, 0, 0)),
                   pl.BlockSpec((None, 8, CONV_CH), lambda b, t: (b, 0, 0))],
        out_shape=[jax.ShapeDtypeStruct((N_PROMPT, D_MODEL), F32),
                   jax.ShapeDtypeStruct((BATCH, HEADS, HEAD_DIM, HEAD_DIM), F32),
                   jax.ShapeDtypeStruct((BATCH, 8, CONV_CH), F32)],
        scratch_shapes=[pltpu.VMEM((TOKEN_TILE, D_RET), BF16)],
        compiler_params=pltpu.CompilerParams(
            dimension_semantics=("arbitrary", "arbitrary"), vmem_limit_bytes=VMEM_LIMIT),
        name="mix_prompt",
    )(hp, *weights, *tables)


def _mixs_kernel(x_ref, g2_ref, g3_ref, win_ref, gn_ref, wret_ref, cw_ref, wconv_ref, wo_ref,
                 cos_ref, sin_ref, bdm_ref, qdec_ref, kdec_ref, cdec_ref, scx_ref, st_ref,
                 *rest):
    (o_ref, sret_ref, ctail_ref,
     q_ref, kd_ref, v_ref, oa_ref, g_ref, ub_ref, gates_ref) = rest[-10:]
    i = pl.program_id(0)
    group_rows = SAMPLE_GROUP * DEC_SEQ

    @pl.when(i == 0)
    def _prologue():
        x = x_ref[...]
        xn = _rms(x, g2_ref[...]).astype(BF16)
        cos = cos_ref[...]
        sin = sin_ref[...]
        qk = _dot(xn, win_ref[:, OFF_QK:OFF_VG])
        vg = _dot(xn, win_ref[:, OFF_VG:OFF_CONV])
        v_ref[...] = vg[:, :D_RET]
        g_ref[...] = vg[:, D_RET:]
        for h in range(HEADS):
            hs = slice(h * HEAD_DIM, (h + 1) * HEAD_DIM)
            ks = slice(D_RET + h * HEAD_DIM, D_RET + (h + 1) * HEAD_DIM)
            qh = _rotary(qk[:, hs], cos, sin)
            kh = _rotary(qk[:, ks], cos, sin) * K_SCALE
            q_ref[:, hs] = qh
            for c in range(N_SAMPLE // SAMPLE_BLOCK):
                rs = slice(c * SAMPLE_BLOCK, (c + 1) * SAMPLE_BLOCK)
                kc = kh[rs]
                scores = _dot_nt(qh[rs].astype(BF16), kc.astype(BF16)) * bdm_ref[h]
                oa_ref[rs, hs] = _dot(scores.astype(BF16), vg[rs, hs].astype(BF16))
                kd_ref[rs, hs] = kc * kdec_ref[h]
        cv = _dot(xn, win_ref[:, OFF_CONV:OFF_GATE])
        cin = cv[:, CONV_CH:2 * CONV_CH] * cv[:, 2 * CONV_CH:]
        ctail_ref[...] = cin
        scx = scx_ref[...]
        tpos = lax.broadcasted_iota(jnp.int32, (N_SAMPLE, CONV_CH), 0) & (DEC_SEQ - 1)
        prev1 = pltpu.roll(scx, N_SAMPLE - 1, 0)
        conv = _short_conv(cin, cw_ref, scx, prev1, tpos)
        ub_ref[...] = _dot((cv[:, :CONV_CH] * conv).astype(BF16), wconv_ref[...])
        gates_ref[...] = _dot(xn, win_ref[:, OFF_GATE:])

    r0 = pl.multiple_of(i * group_rows, group_rows)
    first = lax.broadcasted_iota(jnp.int32, (8, HEAD_DIM), 0) < DEC_SEQ
    for h in range(HEADS):
        hs = slice(h * HEAD_DIM, (h + 1) * HEAD_DIM)
        cdec = cdec_ref[h][0:1, :]
        for j in range(SAMPLE_GROUP // 2):
            rt = pl.ds(pl.multiple_of(r0 + j * 8, 8), 8)
            qt = q_ref[rt, hs].astype(BF16)
            kt = kd_ref[rt, hs]
            vt = v_ref[rt, hs].astype(BF16)
            sa = st_ref[2 * j, h]
            sb = st_ref[2 * j + 1, h]
            cross = jnp.where(first, _dot(qt, sa.astype(BF16)), _dot(qt, sb.astype(BF16)))
            oa_ref[rt, hs] = oa_ref[rt, hs] + cross * qdec_ref[h]
            ka = jnp.where(first, kt, 0.0).astype(BF16)
            kb = jnp.where(first, 0.0, kt).astype(BF16)
            sret_ref[2 * j, h] = cdec * sa + _dot_tn(ka, vt)
            sret_ref[2 * j + 1, h] = cdec * sb + _dot_tn(kb, vt)

    @pl.when(i == pl.num_programs(0) - 1)
    def _epilogue():
        for h in range(HEADS):
            hs = slice(h * HEAD_DIM, (h + 1) * HEAD_DIM)
            on = _group_norm(oa_ref[:, hs], gn_ref[:, hs])
            g_ref[:, hs] = _silu(g_ref[:, hs]) * on
        ua = _dot(g_ref[...].astype(BF16), wret_ref[...])
        o_ref[...] = _merge_out(x_ref[...], ua, ub_ref[...], gates_ref[...], wo_ref, g3_ref)


def _mix_sample(hs, weights, tables, scx, state_ret, ret_acc, layer):
    row = pl.BlockSpec((N_SAMPLE, D_MODEL), lambda i: (0, 0))
    st_block = (None, SAMPLE_GROUP, HEADS, HEAD_DIM, HEAD_DIM)
    st_spec = pl.BlockSpec(st_block, lambda i: (layer, i, 0, 0, 0))
    in_specs = [row] + _mixer_weight_specs() + [
        _whole((N_SAMPLE, HEAD_DIM)), _whole((N_SAMPLE, HEAD_DIM)),
        _whole((HEADS, SAMPLE_BLOCK, SAMPLE_BLOCK)), _whole((HEADS, 8, HEAD_DIM)),
        _whole((HEADS, SAMPLE_BLOCK, HEAD_DIM)), _whole((HEADS, 8, HEAD_DIM)),
        _whole((N_SAMPLE, CONV_CH)), st_spec]
    args = [hs, *weights, *tables, scx, state_ret]
    aliases = {}
    if ret_acc is not None:
        aliases = {len(args): 1}
        in_specs.append(pl.BlockSpec(memory_space=pl.ANY))
        args.append(ret_acc)
    return pl.pallas_call(
        _mixs_kernel,
        grid=(DEC_BATCH // SAMPLE_GROUP,),
        in_specs=in_specs,
        out_specs=[row, st_spec, pl.BlockSpec((N_SAMPLE, CONV_CH), lambda i: (0, 0))],
        out_shape=[jax.ShapeDtypeStruct((N_SAMPLE, D_MODEL), F32),
                   jax.ShapeDtypeStruct((DEPTH, DEC_BATCH, HEADS, HEAD_DIM, HEAD_DIM), F32),
                   jax.ShapeDtypeStruct((N_SAMPLE, CONV_CH), F32)],
        scratch_shapes=[pltpu.VMEM((N_SAMPLE, D_RET), F32),
                        pltpu.VMEM((N_SAMPLE, D_RET), F32),
                        pltpu.VMEM((N_SAMPLE, D_RET), F32),
                        pltpu.VMEM((N_SAMPLE, D_RET), F32),
                        pltpu.VMEM((N_SAMPLE, D_RET), F32),
                        pltpu.VMEM((N_SAMPLE, D_MODEL), F32),
                        pltpu.VMEM((N_SAMPLE, 2 * D_MODEL), F32)],
        input_output_aliases=aliases,
        compiler_params=pltpu.CompilerParams(
            dimension_semantics=("arbitrary",), vmem_limit_bytes=VMEM_LIMIT),
        name="mix_sample",
    )(*args)


def _rope_tables(pos):
    inv_freq = ROPE_BASE ** (-jnp.arange(0, HEAD_DIM, 2, dtype=F32) / HEAD_DIM)
    ang = pos[:, None] * inv_freq[None, :]
    cos = jnp.cos(ang)
    sin = jnp.sin(ang)
    return jnp.concatenate([cos, cos], -1), jnp.concatenate([-sin, sin], -1)


def _decay_tables(chunk):
    log_gamma = jnp.log1p(-jnp.exp2(-5.0 - jnp.arange(HEADS, dtype=F32)))
    idx = jnp.arange(chunk, dtype=F32)
    diff = idx[:, None] - idx[None, :]
    dmask = jnp.where(diff[None] >= 0,
                      jnp.exp(log_gamma[:, None, None] * jnp.maximum(diff, 0.0)[None]), 0.0)
    qdec = jnp.exp(log_gamma[:, None] * (idx[None] + 1.0))
    kdec = jnp.exp(log_gamma[:, None] * (chunk - 1.0 - idx[None]))
    cdec = jnp.exp(log_gamma * chunk)
    return dmask, qdec, kdec, cdec


def _lanes(t, reps):
    t = jnp.tile(t, (1, reps))
    return jnp.broadcast_to(t[:, :, None], t.shape + (HEAD_DIM,))


def _prompt_tables():
    cos, sin = _rope_tables(jnp.arange(SEQ, dtype=F32))
    dmask, qdec, kdec, cdec = _decay_tables(RET_CHUNK)
    cdec = jnp.broadcast_to(cdec[:, None, None], (HEADS, 8, HEAD_DIM))
    return cos, sin, dmask, _lanes(qdec, 1), _lanes(kdec, 1), cdec


def _sample_tables():
    cos, sin = _rope_tables(PAST_LEN + jnp.arange(DEC_SEQ, dtype=F32))
    cos = jnp.tile(cos, (DEC_BATCH, 1))
    sin = jnp.tile(sin, (DEC_BATCH, 1))
    dmask, qdec, kdec, cdec = _decay_tables(DEC_SEQ)
    seqs_per_block = SAMPLE_BLOCK // DEC_SEQ
    eye = jnp.eye(seqs_per_block, dtype=F32)
    bdm = jnp.einsum("ab,hij->haibj", eye, dmask).reshape(HEADS, SAMPLE_BLOCK, SAMPLE_BLOCK)
    cdec = jnp.broadcast_to(cdec[:, None, None], (HEADS, 8, HEAD_DIM))
    return cos, sin, bdm, _lanes(qdec, 8 // DEC_SEQ), _lanes(kdec, seqs_per_block), cdec


def kernel(x_prompt, x_sample, state_ret, state_conv, p_prompt, p_sample, norm_g, w_ffn1_in,
           w_ffn1_out, w_in, ret_gn, w_ret_out, conv_w, w_conv_out, w_o, w_ffn2_in, w_ffn2_out,
           w_ple_gate, w_ple):
    hp = x_prompt.reshape(N_PROMPT, D_MODEL)
    hs = x_sample.reshape(N_SAMPLE, D_MODEL)
    pp = p_prompt.reshape(DEPTH, N_PROMPT, PLE_DIM)
    ps = p_sample.reshape(DEPTH, N_SAMPLE, PLE_DIM)
    tables_p = _prompt_tables()
    tables_s = _sample_tables()

    ret_p, conv_p, conv_s = [], [], []
    ret_s = None
    for i in range(DEPTH):
        g = norm_g[i][:, None, :]
        cw = jnp.concatenate([conv_w[i], jnp.zeros((8 - conv_w.shape[1], CONV_CH), F32)], 0)
        mixer_w = (g[2], g[3], w_in[i].astype(BF16), ret_gn[i][None, :],
                   w_ret_out[i].astype(BF16), cw, w_conv_out[i].astype(BF16),
                   w_o[i].astype(BF16))
        scx = jnp.concatenate([state_conv[i], jnp.zeros_like(state_conv[i])], axis=1)
        scx = scx.reshape(N_SAMPLE, CONV_CH)

        hp, hs = _ffn(hp, hs, g[0], g[1], w_ffn1_in[i].astype(BF16), w_ffn1_out[i].astype(BF16))
        hp, rp, cp = _mix_prompt(hp, mixer_w, tables_p)
        hs, ret_s, ct = _mix_sample(hs, mixer_w, tables_s, scx, state_ret, ret_s, i)
        hp, hs = _ffn_ple(hp, hs, pp, ps, i, g[4], g[5], w_ffn2_in[i].astype(BF16),
                          w_ffn2_out[i].astype(BF16), g[6], g[7],
                          w_ple_gate[i].astype(BF16), w_ple[i].astype(BF16))
        ret_p.append(rp)
        conv_p.append(cp[:, 8 - 2:, :])
        conv_s.append(ct.reshape(DEC_BATCH, DEC_SEQ, CONV_CH)[:, DEC_SEQ - 2:, :])

    return (hp.reshape(BATCH, SEQ, D_MODEL), hs.reshape(DEC_BATCH, DEC_SEQ, D_MODEL),
            jnp.stack(ret_p), jnp.stack(conv_p), ret_s, jnp.stack(conv_s))
```

```python
import jax
import jax.numpy as jnp
from jax import lax
from jax.experimental import pallas as pl
from jax.experimental.pallas import tpu as pltpu

F32 = jnp.float32
BF16 = jnp.bfloat16

D_MODEL = 1024
BATCH = 8
SEQ = 2048
DEPTH = 4
DEC_BATCH = 128
DEC_SEQ = 4
PAST_LEN = 16384
HEADS = 4
HEAD_DIM = 128
D_RET = HEADS * HEAD_DIM
CONV_CH = 512
CONV_W = 3
D_FF = 2816
PLE_DIM = 256
N_NORMS = 8
ROPE_BASE = 10000.0
EPS = 1e-6
K_SCALE = HEAD_DIM ** -0.5

N_PROMPT = BATCH * SEQ
N_SAMPLE = DEC_BATCH * DEC_SEQ

OFF_QK = 0
OFF_VG = 2 * D_RET
OFF_CONV = 4 * D_RET
OFF_GATE = 4 * D_RET + 3 * CONV_CH
N_IN = OFF_GATE + 2 * D_MODEL

TOKEN_TILE = N_SAMPLE
N_PROMPT_TILES = N_PROMPT // TOKEN_TILE
TAIL_SPLIT = 2
RET_CHUNK = 256
FF_CHUNK = 256
N_FF_CHUNKS = D_FF // FF_CHUNK
SAMPLE_GROUP = 8
SAMPLE_BLOCK = 128
VMEM_LIMIT = 56 * 1024 * 1024


def _dot(a, b):
    return jnp.dot(a, b, preferred_element_type=F32)


def _dot_nt(a, b):
    return lax.dot_general(a, b, (((1,), (1,)), ((), ())), preferred_element_type=F32)


def _dot_tn(a, b):
    return lax.dot_general(a, b, (((0,), (0,)), ((), ())), preferred_element_type=F32)


def _rms(x, g_ref, j):
    ms = jnp.mean(x * x, axis=-1, keepdims=True)
    return x * lax.rsqrt(ms + EPS) * g_ref[j:j + 1, :]


def _silu(x):
    return x * jax.nn.sigmoid(x)


def _rotary(t, cos, sin_signed):
    return t * cos + pltpu.roll(t, HEAD_DIM // 2, 1) * sin_signed


def _group_norm(o, gn):
    mu = jnp.mean(o, axis=-1, keepdims=True)
    d = o - mu
    var = jnp.mean(d * d, axis=-1, keepdims=True)
    return d * lax.rsqrt(var + EPS) * gn


def _tail_blocks(rows):
    step = rows // TAIL_SPLIT
    return [slice(r * step, (r + 1) * step) for r in range(TAIL_SPLIT)]


def _whole(shape):
    zeros = (0,) * len(shape)
    return pl.BlockSpec(shape, lambda *_: zeros, pipeline_mode=pl.Buffered(1))


def _layer(shape, layer):
    index = (layer,) + (0,) * len(shape)
    return pl.BlockSpec((None,) + tuple(shape), lambda *_: index, pipeline_mode=pl.Buffered(1))


def _prompt_rows(width, layer=None):
    if layer is None:
        return pl.BlockSpec((TOKEN_TILE, width),
                            lambda i: (jnp.minimum(i, N_PROMPT_TILES - 1), 0))
    return pl.BlockSpec((None, TOKEN_TILE, width),
                        lambda i: (layer, jnp.minimum(i, N_PROMPT_TILES - 1), 0))


def _sample_rows(width, layer=None):
    if layer is None:
        return pl.BlockSpec((N_SAMPLE, width), lambda i: (0, 0))
    return pl.BlockSpec((None, N_SAMPLE, width), lambda i: (layer, 0, 0))


def _ffn_rows(x_ref, p_ref, o_ref, g_ref, wi_ref, wo_ref, wg_ref, we_ref, act_ref, first_norm):
    x = x_ref[...]
    xn = _rms(x, g_ref, first_norm).astype(BF16)
    for c in range(N_FF_CHUNKS):
        cs = slice(c * FF_CHUNK, (c + 1) * FF_CHUNK)
        us = slice(D_FF + c * FF_CHUNK, D_FF + (c + 1) * FF_CHUNK)
        act = _silu(_dot(xn, wi_ref[:, cs])) * _dot(xn, wi_ref[:, us])
        act_ref[:, cs] = act.astype(BF16)
    for rows in _tail_blocks(TOKEN_TILE):
        y = _dot(act_ref[rows, :], wo_ref[...])
        xr = x[rows] + 0.5 * _rms(y, g_ref, first_norm + 1)
        if p_ref is not None:
            gate = jax.nn.sigmoid(_dot(_rms(xr, g_ref, 6).astype(BF16), wg_ref[...]))
            emb = _dot(p_ref[rows, :].astype(BF16), we_ref[...])
            xr = xr + _rms(emb * gate, g_ref, 7)
        o_ref[rows, :] = xr


def _ffn_kernel(xp_ref, xs_ref, g_ref, wi_ref, wo_ref, op_ref, os_ref, act_ref):
    i = pl.program_id(0)

    @pl.when(i < N_PROMPT_TILES)
    def _():
        _ffn_rows(xp_ref, None, op_ref, g_ref, wi_ref, wo_ref, None, None, act_ref, 0)

    @pl.when(i == N_PROMPT_TILES)
    def _():
        _ffn_rows(xs_ref, None, os_ref, g_ref, wi_ref, wo_ref, None, None, act_ref, 0)


def _ffn_ple_kernel(xp_ref, xs_ref, pp_ref, ps_ref, g_ref, wi_ref, wo_ref, wg_ref, we_ref,
                    op_ref, os_ref, act_ref):
    i = pl.program_id(0)

    @pl.when(i < N_PROMPT_TILES)
    def _():
        _ffn_rows(xp_ref, pp_ref, op_ref, g_ref, wi_ref, wo_ref, wg_ref, we_ref, act_ref, 4)

    @pl.when(i == N_PROMPT_TILES)
    def _():
        _ffn_rows(xs_ref, ps_ref, os_ref, g_ref, wi_ref, wo_ref, wg_ref, we_ref, act_ref, 4)


def _token_call(body, name, hp, hs, extra_specs, extra_args):
    return pl.pallas_call(
        body,
        grid=(N_PROMPT_TILES + 1,),
        in_specs=[_prompt_rows(D_MODEL), _sample_rows(D_MODEL)] + extra_specs,
        out_specs=[_prompt_rows(D_MODEL), _sample_rows(D_MODEL)],
        out_shape=[jax.ShapeDtypeStruct((N_PROMPT, D_MODEL), F32),
                   jax.ShapeDtypeStruct((N_SAMPLE, D_MODEL), F32)],
        scratch_shapes=[pltpu.VMEM((TOKEN_TILE, D_FF), BF16)],
        compiler_params=pltpu.CompilerParams(
            dimension_semantics=("arbitrary",), vmem_limit_bytes=VMEM_LIMIT),
        name=name,
    )(hp, hs, *extra_args)


def _ffn(hp, hs, layer, gains, wi, wo):
    specs = [_layer((N_NORMS, D_MODEL), layer), _layer((D_MODEL, 2 * D_FF), layer),
             _layer((D_FF, D_MODEL), layer)]
    return _token_call(_ffn_kernel, "ffn", hp, hs, specs, (gains, wi, wo))


def _ffn_ple(hp, hs, pp, ps, layer, gains, wi, wo, wg, we):
    specs = [_prompt_rows(PLE_DIM, layer), _sample_rows(PLE_DIM, layer),
             _layer((N_NORMS, D_MODEL), layer), _layer((D_MODEL, 2 * D_FF), layer),
             _layer((D_FF, D_MODEL), layer), _layer((D_MODEL, D_MODEL), layer),
             _layer((PLE_DIM, D_MODEL), layer)]
    return _token_call(_ffn_ple_kernel, "ffn_ple", hp, hs, specs, (pp, ps, gains, wi, wo, wg, we))


def _short_conv(cin, cw_ref, prev2, prev1, rows):
    sh1 = jnp.where(rows < 1, prev1, pltpu.roll(cin, 1, 0))
    sh2 = jnp.where(rows < 2, prev2, pltpu.roll(cin, 2, 0))
    return sh2 * cw_ref[0:1, :] + sh1 * cw_ref[1:2, :] + cin * cw_ref[2:3, :]


def _merge_out(x, ua, ub, gates, wo_ref, g_ref):
    merged = jax.nn.sigmoid(gates[:, :D_MODEL]) * ua + jax.nn.sigmoid(gates[:, D_MODEL:]) * ub
    mix = _dot(merged.astype(BF16), wo_ref[...])
    return x + _rms(mix, g_ref, 3)


def _mixer_weight_specs(layer):
    return [_layer((N_NORMS, D_MODEL), layer), _layer((D_MODEL, N_IN), layer),
            _layer((1, D_RET), layer), _layer((D_RET, D_MODEL), layer),
            _layer((CONV_W, CONV_CH), layer), _layer((CONV_CH, D_MODEL), layer),
            _layer((D_MODEL, D_MODEL), layer)]


def _mixp_kernel(x_ref, g_ref, win_ref, gn_ref, wret_ref, cw_ref, wconv_ref, wo_ref,
                 cos_ref, sin_ref, dmask_ref, qdec_ref, kdec_ref, cdec_ref,
                 o_ref, sret_ref, sconv_ref, ya_ref):
    t = pl.program_id(1)

    @pl.when(t == 0)
    def _():
        sret_ref[...] = jnp.zeros_like(sret_ref)
        sconv_ref[...] = jnp.zeros_like(sconv_ref)

    x = x_ref[...]
    xn = _rms(x, g_ref, 2).astype(BF16)
    cos = cos_ref[...]
    sin = sin_ref[...]

    qk = _dot(xn, win_ref[:, OFF_QK:OFF_VG])
    vg = _dot(xn, win_ref[:, OFF_VG:OFF_CONV])
    cv = _dot(xn, win_ref[:, OFF_CONV:OFF_GATE])
    gates = _dot(xn, win_ref[:, OFF_GATE:])

    cin = cv[:, CONV_CH:2 * CONV_CH] * cv[:, 2 * CONV_CH:]
    rows = lax.broadcasted_iota(jnp.int32, (TOKEN_TILE, CONV_CH), 0)
    prev1 = sconv_ref[7:8, :]
    prev2 = jnp.where(rows < 1, sconv_ref[6:7, :], prev1)
    conv = _short_conv(cin, cw_ref, prev2, prev1, rows)
    sconv_ref[...] = cin[TOKEN_TILE - 8:, :]
    ub = _dot((cv[:, :CONV_CH] * conv).astype(BF16), wconv_ref[...])

    for h in range(HEADS):
        hs = slice(h * HEAD_DIM, (h + 1) * HEAD_DIM)
        ks = slice(D_RET + h * HEAD_DIM, D_RET + (h + 1) * HEAD_DIM)
        qh = _rotary(qk[:, hs], cos, sin)
        kh = _rotary(qk[:, ks], cos, sin) * K_SCALE
        state = sret_ref[h]
        for c in range(TOKEN_TILE // RET_CHUNK):
            rs = slice(c * RET_CHUNK, (c + 1) * RET_CHUNK)
            qc = qh[rs].astype(BF16)
            kc = kh[rs]
            vc = vg[rs, hs].astype(BF16)
            scores = _dot_nt(qc, kc.astype(BF16)) * dmask_ref[h]
            o = _dot(scores.astype(BF16), vc) + _dot(qc, state.astype(BF16)) * qdec_ref[h]
            kd = (kc * kdec_ref[h]).astype(BF16)
            state = cdec_ref[h][0:1, :] * state + _dot_tn(kd, vc)
            on = _group_norm(o, gn_ref[:, hs])
            ya_ref[rs, hs] = (_silu(vg[rs, ks]) * on).astype(BF16)
        sret_ref[h] = state
    ua = _dot(ya_ref[...], wret_ref[...])

    for rb in _tail_blocks(TOKEN_TILE):
        o_ref[rb, :] = _merge_out(x[rb], ua[rb], ub[rb], gates[rb], wo_ref, g_ref)


def _mix_prompt(hp, layer, weights, tables):
    nt = SEQ // TOKEN_TILE
    row = pl.BlockSpec((TOKEN_TILE, D_MODEL), lambda b, t: (b * nt + t, 0))
    pos = pl.BlockSpec((TOKEN_TILE, HEAD_DIM), lambda b, t: (t, 0))
    return pl.pallas_call(
        _mixp_kernel,
        grid=(BATCH, nt),
        in_specs=[row] + _mixer_weight_specs(layer) + [
            pos, pos,
            _whole((HEADS, RET_CHUNK, RET_CHUNK)), _whole((HEADS, RET_CHUNK, HEAD_DIM)),
            _whole((HEADS, RET_CHUNK, HEAD_DIM)), _whole((HEADS, 8, HEAD_DIM))],
        out_specs=[row,
                   pl.BlockSpec((None, HEADS, HEAD_DIM, HEAD_DIM), lambda b, t: (b, 0, 0, 0)),
                   pl.BlockSpec((None, 8, CONV_CH), lambda b, t: (b, 0, 0))],
        out_shape=[jax.ShapeDtypeStruct((N_PROMPT, D_MODEL), F32),
                   jax.ShapeDtypeStruct((BATCH, HEADS, HEAD_DIM, HEAD_DIM), F32),
                   jax.ShapeDtypeStruct((BATCH, 8, CONV_CH), F32)],
        scratch_shapes=[pltpu.VMEM((TOKEN_TILE, D_RET), BF16)],
        compiler_params=pltpu.CompilerParams(
            dimension_semantics=("arbitrary", "arbitrary"), vmem_limit_bytes=VMEM_LIMIT),
        name="mix_prompt",
    )(hp, *weights, *tables)


def _mixs_kernel(x_ref, g_ref, win_ref, gn_ref, wret_ref, cw_ref, wconv_ref, wo_ref,
                 cos_ref, sin_ref, bdm_ref, qdec_ref, kdec_ref, cdec_ref, scx_ref, st_ref,
                 *rest):
    (o_ref, sret_ref, ctail_ref,
     q_ref, kd_ref, v_ref, oa_ref, g_scr, ub_ref, gates_ref) = rest[-10:]
    i = pl.program_id(0)
    group_rows = SAMPLE_GROUP * DEC_SEQ

    @pl.when(i == 0)
    def _prologue():
        x = x_ref[...]
        xn = _rms(x, g_ref, 2).astype(BF16)
        cos = cos_ref[...]
        sin = sin_ref[...]
        qk = _dot(xn, win_ref[:, OFF_QK:OFF_VG])
        vg = _dot(xn, win_ref[:, OFF_VG:OFF_CONV])
        v_ref[...] = vg[:, :D_RET]
        g_scr[...] = vg[:, D_RET:]
        for h in range(HEADS):
            hs = slice(h * HEAD_DIM, (h + 1) * HEAD_DIM)
            ks = slice(D_RET + h * HEAD_DIM, D_RET + (h + 1) * HEAD_DIM)
            qh = _rotary(qk[:, hs], cos, sin)
            kh = _rotary(qk[:, ks], cos, sin) * K_SCALE
            q_ref[:, hs] = qh
            for c in range(N_SAMPLE // SAMPLE_BLOCK):
                rs = slice(c * SAMPLE_BLOCK, (c + 1) * SAMPLE_BLOCK)
                kc = kh[rs]
                scores = _dot_nt(qh[rs].astype(BF16), kc.astype(BF16)) * bdm_ref[h]
                oa_ref[rs, hs] = _dot(scores.astype(BF16), vg[rs, hs].astype(BF16))
                kd_ref[rs, hs] = kc * kdec_ref[h]
        cv = _dot(xn, win_ref[:, OFF_CONV:OFF_GATE])
        cin = cv[:, CONV_CH:2 * CONV_CH] * cv[:, 2 * CONV_CH:]
        ctail_ref[...] = cin
        scx = scx_ref[...]
        tpos = lax.broadcasted_iota(jnp.int32, (N_SAMPLE, CONV_CH), 0) & (DEC_SEQ - 1)
        prev1 = pltpu.roll(scx, N_SAMPLE - 1, 0)
        conv = _short_conv(cin, cw_ref, scx, prev1, tpos)
        ub_ref[...] = _dot((cv[:, :CONV_CH] * conv).astype(BF16), wconv_ref[...])
        gates_ref[...] = _dot(xn, win_ref[:, OFF_GATE:])

    r0 = pl.multiple_of(i * group_rows, group_rows)
    first = lax.broadcasted_iota(jnp.int32, (8, HEAD_DIM), 0) < DEC_SEQ
    for h in range(HEADS):
        hs = slice(h * HEAD_DIM, (h + 1) * HEAD_DIM)
        cdec = cdec_ref[h][0:1, :]
        for j in range(SAMPLE_GROUP // 2):
            rt = pl.ds(pl.multiple_of(r0 + j * 8, 8), 8)
            qt = q_ref[rt, hs].astype(BF16)
            kt = kd_ref[rt, hs]
            vt = v_ref[rt, hs].astype(BF16)
            sa = st_ref[2 * j, h]
            sb = st_ref[2 * j + 1, h]
            cross = jnp.where(first, _dot(qt, sa.astype(BF16)), _dot(qt, sb.astype(BF16)))
            oa_ref[rt, hs] = oa_ref[rt, hs] + cross * qdec_ref[h]
            ka = jnp.where(first, kt, 0.0).astype(BF16)
            kb = jnp.where(first, 0.0, kt).astype(BF16)
            sret_ref[2 * j, h] = cdec * sa + _dot_tn(ka, vt)
            sret_ref[2 * j + 1, h] = cdec * sb + _dot_tn(kb, vt)

    @pl.when(i == pl.num_programs(0) - 1)
    def _epilogue():
        for h in range(HEADS):
            hs = slice(h * HEAD_DIM, (h + 1) * HEAD_DIM)
            on = _group_norm(oa_ref[:, hs], gn_ref[:, hs])
            g_scr[:, hs] = _silu(g_scr[:, hs]) * on
        ua = _dot(g_scr[...].astype(BF16), wret_ref[...])
        for rb in _tail_blocks(N_SAMPLE):
            o_ref[rb, :] = _merge_out(x_ref[rb, :], ua[rb], ub_ref[rb, :], gates_ref[rb, :],
                                      wo_ref, g_ref)


def _mix_sample(hs, layer, weights, tables, scx, state_ret, ret_acc):
    row = pl.BlockSpec((N_SAMPLE, D_MODEL), lambda i: (0, 0))
    st_block = (None, SAMPLE_GROUP, HEADS, HEAD_DIM, HEAD_DIM)
    st_spec = pl.BlockSpec(st_block, lambda i: (layer, i, 0, 0, 0))
    in_specs = [row] + _mixer_weight_specs(layer) + [
        _whole((N_SAMPLE, HEAD_DIM)), _whole((N_SAMPLE, HEAD_DIM)),
        _whole((HEADS, SAMPLE_BLOCK, SAMPLE_BLOCK)), _whole((HEADS, 8, HEAD_DIM)),
        _whole((HEADS, SAMPLE_BLOCK, HEAD_DIM)), _whole((HEADS, 8, HEAD_DIM)),
        _layer((N_SAMPLE, CONV_CH), layer), st_spec]
    args = [hs, *weights, *tables, scx, state_ret]
    aliases = {}
    if ret_acc is not None:
        aliases = {len(args): 1}
        in_specs.append(pl.BlockSpec(memory_space=pl.ANY))
        args.append(ret_acc)
    return pl.pallas_call(
        _mixs_kernel,
        grid=(DEC_BATCH // SAMPLE_GROUP,),
        in_specs=in_specs,
        out_specs=[row, st_spec, pl.BlockSpec((N_SAMPLE, CONV_CH), lambda i: (0, 0))],
        out_shape=[jax.ShapeDtypeStruct((N_SAMPLE, D_MODEL), F32),
                   jax.ShapeDtypeStruct((DEPTH, DEC_BATCH, HEADS, HEAD_DIM, HEAD_DIM), F32),
                   jax.ShapeDtypeStruct((N_SAMPLE, CONV_CH), F32)],
        scratch_shapes=[pltpu.VMEM((N_SAMPLE, D_RET), F32),
                        pltpu.VMEM((N_SAMPLE, D_RET), F32),
                        pltpu.VMEM((N_SAMPLE, D_RET), F32),
                        pltpu.VMEM((N_SAMPLE, D_RET), F32),
                        pltpu.VMEM((N_SAMPLE, D_RET), F32),
                        pltpu.VMEM((N_SAMPLE, D_MODEL), F32),
                        pltpu.VMEM((N_SAMPLE, 2 * D_MODEL), F32)],
        input_output_aliases=aliases,
        compiler_params=pltpu.CompilerParams(
            dimension_semantics=("arbitrary",), vmem_limit_bytes=VMEM_LIMIT),
        name="mix_sample",
    )(*args)


def _rope_tables(pos):
    inv_freq = ROPE_BASE ** (-jnp.arange(0, HEAD_DIM, 2, dtype=F32) / HEAD_DIM)
    ang = pos[:, None] * inv_freq[None, :]
    cos = jnp.cos(ang)
    sin = jnp.sin(ang)
    return jnp.concatenate([cos, cos], -1), jnp.concatenate([-sin, sin], -1)


def _decay_tables(chunk):
    log_gamma = jnp.log1p(-jnp.exp2(-5.0 - jnp.arange(HEADS, dtype=F32)))
    idx = jnp.arange(chunk, dtype=F32)
    diff = idx[:, None] - idx[None, :]
    dmask = jnp.where(diff[None] >= 0,
                      jnp.exp(log_gamma[:, None, None] * jnp.maximum(diff, 0.0)[None]), 0.0)
    qdec = jnp.exp(log_gamma[:, None] * (idx[None] + 1.0))
    kdec = jnp.exp(log_gamma[:, None] * (chunk - 1.0 - idx[None]))
    cdec = jnp.exp(log_gamma * chunk)
    return dmask, qdec, kdec, cdec


def _lanes(t, reps):
    t = jnp.tile(t, (1, reps))
    return jnp.broadcast_to(t[:, :, None], t.shape + (HEAD_DIM,))


def _prompt_tables():
    cos, sin = _rope_tables(jnp.arange(SEQ, dtype=F32))
    dmask, qdec, kdec, cdec = _decay_tables(RET_CHUNK)
    cdec = jnp.broadcast_to(cdec[:, None, None], (HEADS, 8, HEAD_DIM))
    return cos, sin, dmask, _lanes(qdec, 1), _lanes(kdec, 1), cdec


def _sample_tables():
    cos, sin = _rope_tables(PAST_LEN + jnp.arange(DEC_SEQ, dtype=F32))
    cos = jnp.tile(cos, (DEC_BATCH, 1))
    sin = jnp.tile(sin, (DEC_BATCH, 1))
    dmask, qdec, kdec, cdec = _decay_tables(DEC_SEQ)
    seqs_per_block = SAMPLE_BLOCK // DEC_SEQ
    eye = jnp.eye(seqs_per_block, dtype=F32)
    bdm = jnp.einsum("ab,hij->haibj", eye, dmask).reshape(HEADS, SAMPLE_BLOCK, SAMPLE_BLOCK)
    cdec = jnp.broadcast_to(cdec[:, None, None], (HEADS, 8, HEAD_DIM))
    return cos, sin, bdm, _lanes(qdec, 8 // DEC_SEQ), _lanes(kdec, seqs_per_block), cdec


def kernel(x_prompt, x_sample, state_ret, state_conv, p_prompt, p_sample, norm_g, w_ffn1_in,
           w_ffn1_out, w_in, ret_gn, w_ret_out, conv_w, w_conv_out, w_o, w_ffn2_in, w_ffn2_out,
           w_ple_gate, w_ple):
    hp = x_prompt.reshape(N_PROMPT, D_MODEL)
    hs = x_sample.reshape(N_SAMPLE, D_MODEL)
    pp = p_prompt.reshape(DEPTH, N_PROMPT, PLE_DIM)
    ps = p_sample.reshape(DEPTH, N_SAMPLE, PLE_DIM)
    tables_p = _prompt_tables()
    tables_s = _sample_tables()

    wi1, wo1 = w_ffn1_in.astype(BF16), w_ffn1_out.astype(BF16)
    wi2, wo2 = w_ffn2_in.astype(BF16), w_ffn2_out.astype(BF16)
    wg, we = w_ple_gate.astype(BF16), w_ple.astype(BF16)
    mixer_w = (norm_g, w_in.astype(BF16), ret_gn.reshape(DEPTH, 1, D_RET),
               w_ret_out.astype(BF16), conv_w, w_conv_out.astype(BF16), w_o.astype(BF16))
    scx = jnp.concatenate([state_conv, jnp.zeros_like(state_conv)], axis=2)
    scx = scx.reshape(DEPTH, N_SAMPLE, CONV_CH)

    ret_p, conv_p, conv_s = [], [], []
    ret_s = None
    for i in range(DEPTH):
        hp, hs = _ffn(hp, hs, i, norm_g, wi1, wo1)
        hp, rp, cp = _mix_prompt(hp, i, mixer_w, tables_p)
        hs, ret_s, ct = _mix_sample(hs, i, mixer_w, tables_s, scx, state_ret, ret_s)
        hp, hs = _ffn_ple(hp, hs, pp, ps, i, norm_g, wi2, wo2, wg, we)
        ret_p.append(rp)
        conv_p.append(cp[:, 8 - (CONV_W - 1):, :])
        conv_s.append(ct.reshape(DEC_BATCH, DEC_SEQ, CONV_CH)[:, DEC_SEQ - (CONV_W - 1):, :])

    return (hp.reshape(BATCH, SEQ, D_MODEL), hs.reshape(DEC_BATCH, DEC_SEQ, D_MODEL),
            jnp.stack(ret_p), jnp.stack(conv_p), ret_s, jnp.stack(conv_s))
```

```python
import functools

import jax
import jax.numpy as jnp
from jax import lax
from jax.experimental import pallas as pl
from jax.experimental.pallas import tpu as pltpu

F32 = jnp.float32
BF16 = jnp.bfloat16

D_MODEL = 1024
BATCH = 8
SEQ = 2048
DEPTH = 4
DEC_BATCH = 128
DEC_SEQ = 4
PAST_LEN = 16384
HEADS = 4
HEAD_DIM = 128
D_RET = HEADS * HEAD_DIM
CONV_CH = 512
CONV_W = 3
D_FF = 2816
PLE_DIM = 256
N_NORMS = 8
ROPE_BASE = 10000.0
EPS = 1e-6
K_SCALE = HEAD_DIM ** -0.5

N_PROMPT = BATCH * SEQ
N_SAMPLE = DEC_BATCH * DEC_SEQ

OFF_QK = 0
OFF_VG = 2 * D_RET
OFF_CONV = 4 * D_RET
OFF_GATE = 4 * D_RET + 3 * CONV_CH
N_IN = OFF_GATE + 2 * D_MODEL

TOKEN_TILE = N_SAMPLE
N_PROMPT_TILES = N_PROMPT // TOKEN_TILE
TAIL_SPLIT = 2
RET_CHUNK = 256
FF_CHUNK = 256
N_FF_CHUNKS = D_FF // FF_CHUNK
SAMPLE_GROUP = 8
SAMPLE_BLOCK = 128
VMEM_LIMIT = 56 * 1024 * 1024
MIXP_STEPS = N_PROMPT // TOKEN_TILE

CAST_PLAN = (
    ("ffn1_in", D_MODEL, 2 * D_FF, 32),
    ("ffn1_out", D_FF, D_MODEL, 16),
    ("in", D_MODEL, N_IN, 32),
    ("ret_out", D_RET, D_MODEL, 32),
    ("conv_out", CONV_CH, D_MODEL, 32),
    ("o", D_MODEL, D_MODEL, 32),
    ("ffn2_in", D_MODEL, 2 * D_FF, 32),
    ("ffn2_out", D_FF, D_MODEL, 16),
    ("ple_gate", D_MODEL, D_MODEL, 32),
    ("ple", PLE_DIM, D_MODEL, 16),
)


def _dot(a, b):
    return jnp.dot(a, b, preferred_element_type=F32)


def _dot_nt(a, b):
    return lax.dot_general(a, b, (((1,), (1,)), ((), ())), preferred_element_type=F32)


def _dot_tn(a, b):
    return lax.dot_general(a, b, (((0,), (0,)), ((), ())), preferred_element_type=F32)


def _rms(x, g_ref, j):
    ms = jnp.mean(x * x, axis=-1, keepdims=True)
    return x * lax.rsqrt(ms + EPS) * g_ref[j:j + 1, :]


def _silu(x):
    return x * jax.nn.sigmoid(x)


def _rotary(t, cos, sin_signed):
    return t * cos + pltpu.roll(t, HEAD_DIM // 2, 1) * sin_signed


def _group_norm(o, gn):
    mu = jnp.mean(o, axis=-1, keepdims=True)
    d = o - mu
    var = jnp.mean(d * d, axis=-1, keepdims=True)
    return d * lax.rsqrt(var + EPS) * gn


def _tail_blocks(rows):
    step = rows // TAIL_SPLIT
    return [slice(r * step, (r + 1) * step) for r in range(TAIL_SPLIT)]


def _whole(shape):
    zeros = (0,) * len(shape)
    return pl.BlockSpec(shape, lambda *_: zeros, pipeline_mode=pl.Buffered(1))


def _layer(shape, layer):
    index = (layer,) + (0,) * len(shape)
    return pl.BlockSpec((None,) + tuple(shape), lambda *_: index, pipeline_mode=pl.Buffered(1))


def _prompt_rows(width, layer=None):
    if layer is None:
        return pl.BlockSpec((TOKEN_TILE, width),
                            lambda i: (jnp.minimum(i, N_PROMPT_TILES - 1), 0))
    return pl.BlockSpec((None, TOKEN_TILE, width),
                        lambda i: (layer, jnp.minimum(i, N_PROMPT_TILES - 1), 0))


def _sample_rows(width, layer=None):
    if layer is None:
        return pl.BlockSpec((N_SAMPLE, width), lambda i: (0, 0))
    return pl.BlockSpec((None, N_SAMPLE, width), lambda i: (layer, 0, 0))


def _ffn_rows(x_ref, p_ref, o_ref, g_ref, wi_ref, wo_ref, wg_ref, we_ref, act_ref, first_norm):
    x = x_ref[...]
    xn = _rms(x, g_ref, first_norm).astype(BF16)
    for c in range(N_FF_CHUNKS):
        cs = slice(c * FF_CHUNK, (c + 1) * FF_CHUNK)
        us = slice(D_FF + c * FF_CHUNK, D_FF + (c + 1) * FF_CHUNK)
        act = _silu(_dot(xn, wi_ref[:, cs])) * _dot(xn, wi_ref[:, us])
        act_ref[:, cs] = act.astype(BF16)
    for rows in _tail_blocks(TOKEN_TILE):
        y = _dot(act_ref[rows, :], wo_ref[...])
        xr = x[rows] + 0.5 * _rms(y, g_ref, first_norm + 1)
        if p_ref is not None:
            gate = jax.nn.sigmoid(_dot(_rms(xr, g_ref, 6).astype(BF16), wg_ref[...]))
            emb = _dot(p_ref[rows, :].astype(BF16), we_ref[...])
            xr = xr + _rms(emb * gate, g_ref, 7)
        o_ref[rows, :] = xr


def _ffn_kernel(xp_ref, xs_ref, g_ref, wi_ref, wo_ref, op_ref, os_ref, act_ref):
    i = pl.program_id(0)

    @pl.when(i < N_PROMPT_TILES)
    def _():
        _ffn_rows(xp_ref, None, op_ref, g_ref, wi_ref, wo_ref, None, None, act_ref, 0)

    @pl.when(i == N_PROMPT_TILES)
    def _():
        _ffn_rows(xs_ref, None, os_ref, g_ref, wi_ref, wo_ref, None, None, act_ref, 0)


def _ffn_ple_kernel(xp_ref, xs_ref, pp_ref, ps_ref, g_ref, wi_ref, wo_ref, wg_ref, we_ref,
                    op_ref, os_ref, act_ref):
    i = pl.program_id(0)

    @pl.when(i < N_PROMPT_TILES)
    def _():
        _ffn_rows(xp_ref, pp_ref, op_ref, g_ref, wi_ref, wo_ref, wg_ref, we_ref, act_ref, 4)

    @pl.when(i == N_PROMPT_TILES)
    def _():
        _ffn_rows(xs_ref, ps_ref, os_ref, g_ref, wi_ref, wo_ref, wg_ref, we_ref, act_ref, 4)


def _token_call(body, name, hp, hs, extra_specs, extra_args):
    return pl.pallas_call(
        body,
        grid=(N_PROMPT_TILES + 1,),
        in_specs=[_prompt_rows(D_MODEL), _sample_rows(D_MODEL)] + extra_specs,
        out_specs=[_prompt_rows(D_MODEL), _sample_rows(D_MODEL)],
        out_shape=[jax.ShapeDtypeStruct((N_PROMPT, D_MODEL), F32),
                   jax.ShapeDtypeStruct((N_SAMPLE, D_MODEL), F32)],
        scratch_shapes=[pltpu.VMEM((TOKEN_TILE, D_FF), BF16)],
        compiler_params=pltpu.CompilerParams(
            dimension_semantics=("arbitrary",), vmem_limit_bytes=VMEM_LIMIT),
        name=name,
    )(hp, hs, *extra_args)


def _ffn(hp, hs, layer, gains, w):
    specs = [_layer((N_NORMS, D_MODEL), layer), _whole((D_MODEL, 2 * D_FF)),
             _whole((D_FF, D_MODEL))]
    return _token_call(_ffn_kernel, "ffn", hp, hs, specs, (gains, w["ffn1_in"], w["ffn1_out"]))


def _ffn_ple(hp, hs, pp, ps, layer, gains, w):
    specs = [_prompt_rows(PLE_DIM, layer), _sample_rows(PLE_DIM, layer),
             _layer((N_NORMS, D_MODEL), layer), _whole((D_MODEL, 2 * D_FF)),
             _whole((D_FF, D_MODEL)), _whole((D_MODEL, D_MODEL)), _whole((PLE_DIM, D_MODEL))]
    return _token_call(_ffn_ple_kernel, "ffn_ple", hp, hs, specs,
                       (pp, ps, gains, w["ffn2_in"], w["ffn2_out"], w["ple_gate"], w["ple"]))


def _short_conv(cin, cw_ref, prev2, prev1, rows):
    sh1 = jnp.where(rows < 1, prev1, pltpu.roll(cin, 1, 0))
    sh2 = jnp.where(rows < 2, prev2, pltpu.roll(cin, 2, 0))
    return sh2 * cw_ref[0:1, :] + sh1 * cw_ref[1:2, :] + cin * cw_ref[2:3, :]


def _merge_out(x, ua, ub, gates, wo_ref, g_ref):
    merged = jax.nn.sigmoid(gates[:, :D_MODEL]) * ua + jax.nn.sigmoid(gates[:, D_MODEL:]) * ub
    mix = _dot(merged.astype(BF16), wo_ref[...])
    return x + _rms(mix, g_ref, 3)


def _mixer_weight_specs(layer):
    return [_layer((N_NORMS, D_MODEL), layer), _whole((D_MODEL, N_IN)),
            _layer((1, D_RET), layer), _whole((D_RET, D_MODEL)),
            _layer((CONV_W, CONV_CH), layer), _whole((CONV_CH, D_MODEL)),
            _whole((D_MODEL, D_MODEL))]


def _mixer_weights(small, w):
    norm_g, ret_gn, conv_w = small
    return (norm_g, w["in"], ret_gn, w["ret_out"], conv_w, w["conv_out"], w["o"])


def _mixp_kernel(x_ref, g_ref, win_ref, gn_ref, wret_ref, cw_ref, wconv_ref, wo_ref,
                 cos_ref, sin_ref, dmask_ref, qdec_ref, kdec_ref, cdec_ref, *rest, n_cast):
    cast_src = rest[:n_cast]
    o_ref, sret_ref, sconv_ref = rest[n_cast:n_cast + 3]
    cast_dst = rest[n_cast + 3:2 * n_cast + 3]
    ya_ref = rest[-1]
    t = pl.program_id(1)
    step = pl.program_id(0) * pl.num_programs(1) + t

    @pl.when(t == 0)
    def _():
        sret_ref[...] = jnp.zeros_like(sret_ref)
        sconv_ref[...] = jnp.zeros_like(sconv_ref)

    for (_, _, _, blocks), src, dst in zip(CAST_PLAN, cast_src, cast_dst):
        if blocks == MIXP_STEPS:
            dst[...] = src[...].astype(BF16)
        else:
            @pl.when(step < blocks)
            def _(src=src, dst=dst):
                dst[...] = src[...].astype(BF16)

    x = x_ref[...]
    xn = _rms(x, g_ref, 2).astype(BF16)
    cos = cos_ref[...]
    sin = sin_ref[...]

    qk = _dot(xn, win_ref[:, OFF_QK:OFF_VG])
    vg = _dot(xn, win_ref[:, OFF_VG:OFF_CONV])
    cv = _dot(xn, win_ref[:, OFF_CONV:OFF_GATE])
    gates = _dot(xn, win_ref[:, OFF_GATE:])

    cin = cv[:, CONV_CH:2 * CONV_CH] * cv[:, 2 * CONV_CH:]
    rows = lax.broadcasted_iota(jnp.int32, (TOKEN_TILE, CONV_CH), 0)
    prev1 = sconv_ref[7:8, :]
    prev2 = jnp.where(rows < 1, sconv_ref[6:7, :], prev1)
    conv = _short_conv(cin, cw_ref, prev2, prev1, rows)
    sconv_ref[...] = cin[TOKEN_TILE - 8:, :]
    ub = _dot((cv[:, :CONV_CH] * conv).astype(BF16), wconv_ref[...])

    for h in range(HEADS):
        hs = slice(h * HEAD_DIM, (h + 1) * HEAD_DIM)
        ks = slice(D_RET + h * HEAD_DIM, D_RET + (h + 1) * HEAD_DIM)
        qh = _rotary(qk[:, hs], cos, sin)
        kh = _rotary(qk[:, ks], cos, sin) * K_SCALE
        state = sret_ref[h]
        for c in range(TOKEN_TILE // RET_CHUNK):
            rs = slice(c * RET_CHUNK, (c + 1) * RET_CHUNK)
            qc = qh[rs].astype(BF16)
            kc = kh[rs]
            vc = vg[rs, hs].astype(BF16)
            scores = _dot_nt(qc, kc.astype(BF16)) * dmask_ref[h]
            o = _dot(scores.astype(BF16), vc) + _dot(qc, state.astype(BF16)) * qdec_ref[h]
            kd = (kc * kdec_ref[h]).astype(BF16)
            state = cdec_ref[h][0:1, :] * state + _dot_tn(kd, vc)
            on = _group_norm(o, gn_ref[:, hs])
            ya_ref[rs, hs] = (_silu(vg[rs, ks]) * on).astype(BF16)
        sret_ref[h] = state
    ua = _dot(ya_ref[...], wret_ref[...])

    for rb in _tail_blocks(TOKEN_TILE):
        o_ref[rb, :] = _merge_out(x[rb], ua[rb], ub[rb], gates[rb], wo_ref, g_ref)


def _mix_prompt(hp, layer, weights, tables, next_f32):
    nt = SEQ // TOKEN_TILE
    row = pl.BlockSpec((TOKEN_TILE, D_MODEL), lambda b, t: (b * nt + t, 0))
    pos = pl.BlockSpec((TOKEN_TILE, HEAD_DIM), lambda b, t: (t, 0))
    cast_in, cast_out, cast_shapes = [], [], []
    for (_, rows, cols, blocks), _ in zip(CAST_PLAN, next_f32):
        last = blocks - 1
        cast_in.append(pl.BlockSpec(
            (None, rows // blocks, cols),
            lambda b, t, last=last: (layer + 1, jnp.minimum(b * nt + t, last), 0)))
        cast_out.append(pl.BlockSpec(
            (rows // blocks, cols), lambda b, t, last=last: (jnp.minimum(b * nt + t, last), 0)))
        cast_shapes.append(jax.ShapeDtypeStruct((rows, cols), BF16))
    return pl.pallas_call(
        functools.partial(_mixp_kernel, n_cast=len(next_f32)),
        grid=(BATCH, nt),
        in_specs=[row] + _mixer_weight_specs(layer) + [
            pos, pos,
            _whole((HEADS, RET_CHUNK, RET_CHUNK)), _whole((HEADS, RET_CHUNK, HEAD_DIM)),
            _whole((HEADS, RET_CHUNK, HEAD_DIM)), _whole((HEADS, 8, HEAD_DIM))] + cast_in,
        out_specs=[row,
                   pl.BlockSpec((None, HEADS, HEAD_DIM, HEAD_DIM), lambda b, t: (b, 0, 0, 0)),
                   pl.BlockSpec((None, 8, CONV_CH), lambda b, t: (b, 0, 0))] + cast_out,
        out_shape=[jax.ShapeDtypeStruct((N_PROMPT, D_MODEL), F32),
                   jax.ShapeDtypeStruct((BATCH, HEADS, HEAD_DIM, HEAD_DIM), F32),
                   jax.ShapeDtypeStruct((BATCH, 8, CONV_CH), F32)] + cast_shapes,
        scratch_shapes=[pltpu.VMEM((TOKEN_TILE, D_RET), BF16)],
        compiler_params=pltpu.CompilerParams(
            dimension_semantics=("arbitrary", "arbitrary"), vmem_limit_bytes=VMEM_LIMIT),
        name="mix_prompt",
    )(hp, *weights, *tables, *next_f32)


def _mixs_kernel(x_ref, g_ref, win_ref, gn_ref, wret_ref, cw_ref, wconv_ref, wo_ref,
                 cos_ref, sin_ref, bdm_ref, qdec_ref, kdec_ref, cdec_ref, scx_ref, st_ref,
                 *rest):
    (o_ref, sret_ref, ctail_ref,
     q_ref, kd_ref, v_ref, oa_ref, g_scr, ub_ref, gates_ref) = rest[-10:]
    i = pl.program_id(0)
    group_rows = SAMPLE_GROUP * DEC_SEQ

    @pl.when(i == 0)
    def _prologue():
        x = x_ref[...]
        xn = _rms(x, g_ref, 2).astype(BF16)
        cos = cos_ref[...]
        sin = sin_ref[...]
        qk = _dot(xn, win_ref[:, OFF_QK:OFF_VG])
        vg = _dot(xn, win_ref[:, OFF_VG:OFF_CONV])
        v_ref[...] = vg[:, :D_RET]
        g_scr[...] = vg[:, D_RET:]
        for h in range(HEADS):
            hs = slice(h * HEAD_DIM, (h + 1) * HEAD_DIM)
            ks = slice(D_RET + h * HEAD_DIM, D_RET + (h + 1) * HEAD_DIM)
            qh = _rotary(qk[:, hs], cos, sin)
            kh = _rotary(qk[:, ks], cos, sin) * K_SCALE
            q_ref[:, hs] = qh
            for c in range(N_SAMPLE // SAMPLE_BLOCK):
                rs = slice(c * SAMPLE_BLOCK, (c + 1) * SAMPLE_BLOCK)
                kc = kh[rs]
                scores = _dot_nt(qh[rs].astype(BF16), kc.astype(BF16)) * bdm_ref[h]
                oa_ref[rs, hs] = _dot(scores.astype(BF16), vg[rs, hs].astype(BF16))
                kd_ref[rs, hs] = kc * kdec_ref[h]
        cv = _dot(xn, win_ref[:, OFF_CONV:OFF_GATE])
        cin = cv[:, CONV_CH:2 * CONV_CH] * cv[:, 2 * CONV_CH:]
        ctail_ref[...] = cin
        scx = scx_ref[...]
        tpos = lax.broadcasted_iota(jnp.int32, (N_SAMPLE, CONV_CH), 0) & (DEC_SEQ - 1)
        prev1 = pltpu.roll(scx, N_SAMPLE - 1, 0)
        conv = _short_conv(cin, cw_ref, scx, prev1, tpos)
        ub_ref[...] = _dot((cv[:, :CONV_CH] * conv).astype(BF16), wconv_ref[...])
        gates_ref[...] = _dot(xn, win_ref[:, OFF_GATE:])

    r0 = pl.multiple_of(i * group_rows, group_rows)
    first = lax.broadcasted_iota(jnp.int32, (8, HEAD_DIM), 0) < DEC_SEQ
    for h in range(HEADS):
        hs = slice(h * HEAD_DIM, (h + 1) * HEAD_DIM)
        cdec = cdec_ref[h][0:1, :]
        for j in range(SAMPLE_GROUP // 2):
            rt = pl.ds(pl.multiple_of(r0 + j * 8, 8), 8)
            qt = q_ref[rt, hs].astype(BF16)
            kt = kd_ref[rt, hs]
            vt = v_ref[rt, hs].astype(BF16)
            sa = st_ref[2 * j, h]
            sb = st_ref[2 * j + 1, h]
            cross = jnp.where(first, _dot(qt, sa.astype(BF16)), _dot(qt, sb.astype(BF16)))
            oa_ref[rt, hs] = oa_ref[rt, hs] + cross * qdec_ref[h]
            ka = jnp.where(first, kt, 0.0).astype(BF16)
            kb = jnp.where(first, 0.0, kt).astype(BF16)
            sret_ref[2 * j, h] = cdec * sa + _dot_tn(ka, vt)
            sret_ref[2 * j + 1, h] = cdec * sb + _dot_tn(kb, vt)

    @pl.when(i == pl.num_programs(0) - 1)
    def _epilogue():
        for h in range(HEADS):
            hs = slice(h * HEAD_DIM, (h + 1) * HEAD_DIM)
            on = _group_norm(oa_ref[:, hs], gn_ref[:, hs])
            g_scr[:, hs] = _silu(g_scr[:, hs]) * on
        ua = _dot(g_scr[...].astype(BF16), wret_ref[...])
        for rb in _tail_blocks(N_SAMPLE):
            o_ref[rb, :] = _merge_out(x_ref[rb, :], ua[rb], ub_ref[rb, :], gates_ref[rb, :],
                                      wo_ref, g_ref)


def _mix_sample(hs, layer, weights, tables, scx, state_ret, ret_acc):
    row = pl.BlockSpec((N_SAMPLE, D_MODEL), lambda i: (0, 0))
    st_block = (None, SAMPLE_GROUP, HEADS, HEAD_DIM, HEAD_DIM)
    st_spec = pl.BlockSpec(st_block, lambda i: (layer, i, 0, 0, 0))
    in_specs = [row] + _mixer_weight_specs(layer) + [
        _whole((N_SAMPLE, HEAD_DIM)), _whole((N_SAMPLE, HEAD_DIM)),
        _whole((HEADS, SAMPLE_BLOCK, SAMPLE_BLOCK)), _whole((HEADS, 8, HEAD_DIM)),
        _whole((HEADS, SAMPLE_BLOCK, HEAD_DIM)), _whole((HEADS, 8, HEAD_DIM)),
        _layer((N_SAMPLE, CONV_CH), layer), st_spec]
    args = [hs, *weights, *tables, scx, state_ret]
    aliases = {}
    if ret_acc is not None:
        aliases = {len(args): 1}
        in_specs.append(pl.BlockSpec(memory_space=pl.ANY))
        args.append(ret_acc)
    return pl.pallas_call(
        _mixs_kernel,
        grid=(DEC_BATCH // SAMPLE_GROUP,),
        in_specs=in_specs,
        out_specs=[row, st_spec, pl.BlockSpec((N_SAMPLE, CONV_CH), lambda i: (0, 0))],
        out_shape=[jax.ShapeDtypeStruct((N_SAMPLE, D_MODEL), F32),
                   jax.ShapeDtypeStruct((DEPTH, DEC_BATCH, HEADS, HEAD_DIM, HEAD_DIM), F32),
                   jax.ShapeDtypeStruct((N_SAMPLE, CONV_CH), F32)],
        scratch_shapes=[pltpu.VMEM((N_SAMPLE, D_RET), F32),
                        pltpu.VMEM((N_SAMPLE, D_RET), F32),
                        pltpu.VMEM((N_SAMPLE, D_RET), F32),
                        pltpu.VMEM((N_SAMPLE, D_RET), F32),
                        pltpu.VMEM((N_SAMPLE, D_RET), F32),
                        pltpu.VMEM((N_SAMPLE, D_MODEL), F32),
                        pltpu.VMEM((N_SAMPLE, 2 * D_MODEL), F32)],
        input_output_aliases=aliases,
        compiler_params=pltpu.CompilerParams(
            dimension_semantics=("arbitrary",), vmem_limit_bytes=VMEM_LIMIT),
        name="mix_sample",
    )(*args)


def _rope_tables(pos):
    inv_freq = ROPE_BASE ** (-jnp.arange(0, HEAD_DIM, 2, dtype=F32) / HEAD_DIM)
    ang = pos[:, None] * inv_freq[None, :]
    cos = jnp.cos(ang)
    sin = jnp.sin(ang)
    return jnp.concatenate([cos, cos], -1), jnp.concatenate([-sin, sin], -1)


def _decay_tables(chunk):
    log_gamma = jnp.log1p(-jnp.exp2(-5.0 - jnp.arange(HEADS, dtype=F32)))
    idx = jnp.arange(chunk, dtype=F32)
    diff = idx[:, None] - idx[None, :]
    dmask = jnp.where(diff[None] >= 0,
                      jnp.exp(log_gamma[:, None, None] * jnp.maximum(diff, 0.0)[None]), 0.0)
    qdec = jnp.exp(log_gamma[:, None] * (idx[None] + 1.0))
    kdec = jnp.exp(log_gamma[:, None] * (chunk - 1.0 - idx[None]))
    cdec = jnp.exp(log_gamma * chunk)
    return dmask, qdec, kdec, cdec


def _lanes(t, reps):
    t = jnp.tile(t, (1, reps))
    return jnp.broadcast_to(t[:, :, None], t.shape + (HEAD_DIM,))


def _prompt_tables():
    cos, sin = _rope_tables(jnp.arange(SEQ, dtype=F32))
    dmask, qdec, kdec, cdec = _decay_tables(RET_CHUNK)
    cdec = jnp.broadcast_to(cdec[:, None, None], (HEADS, 8, HEAD_DIM))
    return cos, sin, dmask, _lanes(qdec, 1), _lanes(kdec, 1), cdec


def _sample_tables():
    cos, sin = _rope_tables(PAST_LEN + jnp.arange(DEC_SEQ, dtype=F32))
    cos = jnp.tile(cos, (DEC_BATCH, 1))
    sin = jnp.tile(sin, (DEC_BATCH, 1))
    dmask, qdec, kdec, cdec = _decay_tables(DEC_SEQ)
    seqs_per_block = SAMPLE_BLOCK // DEC_SEQ
    eye = jnp.eye(seqs_per_block, dtype=F32)
    bdm = jnp.einsum("ab,hij->haibj", eye, dmask).reshape(HEADS, SAMPLE_BLOCK, SAMPLE_BLOCK)
    cdec = jnp.broadcast_to(cdec[:, None, None], (HEADS, 8, HEAD_DIM))
    return cos, sin, bdm, _lanes(qdec, 8 // DEC_SEQ), _lanes(kdec, seqs_per_block), cdec


def kernel(x_prompt, x_sample, state_ret, state_conv, p_prompt, p_sample, norm_g, w_ffn1_in,
           w_ffn1_out, w_in, ret_gn, w_ret_out, conv_w, w_conv_out, w_o, w_ffn2_in, w_ffn2_out,
           w_ple_gate, w_ple):
    hp = x_prompt.reshape(N_PROMPT, D_MODEL)
    hs = x_sample.reshape(N_SAMPLE, D_MODEL)
    pp = p_prompt.reshape(DEPTH, N_PROMPT, PLE_DIM)
    ps = p_sample.reshape(DEPTH, N_SAMPLE, PLE_DIM)
    tables_p = _prompt_tables()
    tables_s = _sample_tables()

    f32_weights = (w_ffn1_in, w_ffn1_out, w_in, w_ret_out, w_conv_out, w_o, w_ffn2_in,
                   w_ffn2_out, w_ple_gate, w_ple)
    names = [name for name, _, _, _ in CAST_PLAN]
    w = {name: wf[0].astype(BF16) for name, wf in zip(names, f32_weights)}
    small = (norm_g, ret_gn.reshape(DEPTH, 1, D_RET), conv_w)
    scx = jnp.concatenate([state_conv, jnp.zeros_like(state_conv)], axis=2)
    scx = scx.reshape(DEPTH, N_SAMPLE, CONV_CH)

    ret_p, conv_p, conv_s = [], [], []
    ret_s = None
    for i in range(DEPTH):
        hp, hs = _ffn(hp, hs, i, norm_g, w)
        mixer_w = _mixer_weights(small, w)
        hp, rp, cp, *w_next = _mix_prompt(hp, i, mixer_w, tables_p,
                                          f32_weights if i + 1 < DEPTH else ())
        hs, ret_s, ct = _mix_sample(hs, i, mixer_w, tables_s, scx, state_ret, ret_s)
        hp, hs = _ffn_ple(hp, hs, pp, ps, i, norm_g, w)
        if w_next:
            w = dict(zip(names, w_next))
        ret_p.append(rp)
        conv_p.append(cp[:, 8 - (CONV_W - 1):, :])
        conv_s.append(ct.reshape(DEC_BATCH, DEC_SEQ, CONV_CH)[:, DEC_SEQ - (CONV_W - 1):, :])

    return (hp.reshape(BATCH, SEQ, D_MODEL), hs.reshape(DEC_BATCH, DEC_SEQ, D_MODEL),
            jnp.stack(ret_p), jnp.stack(conv_p), ret_s, jnp.stack(conv_s))
```

```python
import functools

import jax
import jax.numpy as jnp
from jax import lax
from jax.experimental import pallas as pl
from jax.experimental.pallas import tpu as pltpu

F32 = jnp.float32
BF16 = jnp.bfloat16

D_MODEL = 1024
BATCH = 8
SEQ = 2048
DEPTH = 4
DEC_BATCH = 128
DEC_SEQ = 4
PAST_LEN = 16384
HEADS = 4
HEAD_DIM = 128
D_RET = HEADS * HEAD_DIM
CONV_CH = 512
CONV_W = 3
D_FF = 2816
PLE_DIM = 256
N_NORMS = 8
ROPE_BASE = 10000.0
EPS = 1e-6
K_SCALE = HEAD_DIM ** -0.5

N_PROMPT = BATCH * SEQ
N_SAMPLE = DEC_BATCH * DEC_SEQ

OFF_QK = 0
OFF_VG = 2 * D_RET
OFF_CONV = 4 * D_RET
OFF_GATE = 4 * D_RET + 3 * CONV_CH
N_IN = OFF_GATE + 2 * D_MODEL

TOKEN_TILE = N_SAMPLE
FFN_TILE = 1024
FFN_PLE_TILE = 1024
TAIL_SPLIT = 2
RET_CHUNK = 256
FF_CHUNK = 256
N_FF_CHUNKS = D_FF // FF_CHUNK
SAMPLE_GROUP = 8
SAMPLE_BLOCK = 128
VMEM_LIMIT = 56 * 1024 * 1024
MIXP_STEPS = N_PROMPT // TOKEN_TILE

CAST_PLAN = (
    ("ffn1_in", D_MODEL, 2 * D_FF, 32),
    ("ffn1_out", D_FF, D_MODEL, 16),
    ("in", D_MODEL, N_IN, 32),
    ("ret_out", D_RET, D_MODEL, 32),
    ("conv_out", CONV_CH, D_MODEL, 32),
    ("o", D_MODEL, D_MODEL, 32),
    ("ffn2_in", D_MODEL, 2 * D_FF, 32),
    ("ffn2_out", D_FF, D_MODEL, 16),
    ("ple_gate", D_MODEL, D_MODEL, 32),
    ("ple", PLE_DIM, D_MODEL, 16),
)


def _dot(a, b):
    return jnp.dot(a, b, preferred_element_type=F32)


def _dot_nt(a, b):
    return lax.dot_general(a, b, (((1,), (1,)), ((), ())), preferred_element_type=F32)


def _dot_tn(a, b):
    return lax.dot_general(a, b, (((0,), (0,)), ((), ())), preferred_element_type=F32)


def _rms(x, g_ref, j):
    ms = jnp.mean(x * x, axis=-1, keepdims=True)
    return x * lax.rsqrt(ms + EPS) * g_ref[j:j + 1, :]


def _silu(x):
    return x * jax.nn.sigmoid(x)


def _rotary(t, cos, sin_signed):
    return t * cos + pltpu.roll(t, HEAD_DIM // 2, 1) * sin_signed


def _group_norm(o, gn):
    mu = jnp.mean(o, axis=-1, keepdims=True)
    d = o - mu
    var = jnp.mean(d * d, axis=-1, keepdims=True)
    return d * lax.rsqrt(var + EPS) * gn


def _tail_blocks(rows):
    step = rows // TAIL_SPLIT
    return [slice(r * step, (r + 1) * step) for r in range(TAIL_SPLIT)]


def _whole(shape):
    zeros = (0,) * len(shape)
    return pl.BlockSpec(shape, lambda *_: zeros, pipeline_mode=pl.Buffered(1))


def _layer(shape, layer):
    index = (layer,) + (0,) * len(shape)
    return pl.BlockSpec((None,) + tuple(shape), lambda *_: index, pipeline_mode=pl.Buffered(1))


def _prompt_rows(width, tile, layer=None):
    last = N_PROMPT // tile - 1
    if layer is None:
        return pl.BlockSpec((tile, width), lambda i: (jnp.minimum(i, last), 0))
    return pl.BlockSpec((None, tile, width), lambda i: (layer, jnp.minimum(i, last), 0))


def _sample_rows(width, layer=None):
    if layer is None:
        return pl.BlockSpec((N_SAMPLE, width), lambda i: (0, 0))
    return pl.BlockSpec((None, N_SAMPLE, width), lambda i: (layer, 0, 0))


def _ffn_rows(x_ref, p_ref, o_ref, g_ref, wi_ref, wo_ref, wg_ref, we_ref, act_ref, first_norm):
    x = x_ref[...]
    n = x.shape[0]
    xn = _rms(x, g_ref, first_norm).astype(BF16)
    for c in range(N_FF_CHUNKS):
        cs = slice(c * FF_CHUNK, (c + 1) * FF_CHUNK)
        us = slice(D_FF + c * FF_CHUNK, D_FF + (c + 1) * FF_CHUNK)
        act = _silu(_dot(xn, wi_ref[:, cs])) * _dot(xn, wi_ref[:, us])
        act_ref[0:n, cs] = act.astype(BF16)
    for rows in _tail_blocks(n):
        y = _dot(act_ref[rows, :], wo_ref[...])
        xr = x[rows] + 0.5 * _rms(y, g_ref, first_norm + 1)
        if p_ref is not None:
            gate = jax.nn.sigmoid(_dot(_rms(xr, g_ref, 6).astype(BF16), wg_ref[...]))
            emb = _dot(p_ref[rows, :].astype(BF16), we_ref[...])
            xr = xr + _rms(emb * gate, g_ref, 7)
        o_ref[rows, :] = xr


def _ffn_kernel(xp_ref, xs_ref, g_ref, wi_ref, wo_ref, op_ref, os_ref, act_ref):
    i = pl.program_id(0)
    sample_step = pl.num_programs(0) - 1

    @pl.when(i < sample_step)
    def _():
        _ffn_rows(xp_ref, None, op_ref, g_ref, wi_ref, wo_ref, None, None, act_ref, 0)

    @pl.when(i == sample_step)
    def _():
        _ffn_rows(xs_ref, None, os_ref, g_ref, wi_ref, wo_ref, None, None, act_ref, 0)


def _ffn_ple_kernel(xp_ref, xs_ref, pp_ref, ps_ref, g_ref, wi_ref, wo_ref, wg_ref, we_ref,
                    op_ref, os_ref, act_ref):
    i = pl.program_id(0)
    sample_step = pl.num_programs(0) - 1

    @pl.when(i < sample_step)
    def _():
        _ffn_rows(xp_ref, pp_ref, op_ref, g_ref, wi_ref, wo_ref, wg_ref, we_ref, act_ref, 4)

    @pl.when(i == sample_step)
    def _():
        _ffn_rows(xs_ref, ps_ref, os_ref, g_ref, wi_ref, wo_ref, wg_ref, we_ref, act_ref, 4)


def _token_call(body, name, tile, hp, hs, extra_specs, extra_args):
    return pl.pallas_call(
        body,
        grid=(N_PROMPT // tile + 1,),
        in_specs=[_prompt_rows(D_MODEL, tile), _sample_rows(D_MODEL)] + extra_specs,
        out_specs=[_prompt_rows(D_MODEL, tile), _sample_rows(D_MODEL)],
        out_shape=[jax.ShapeDtypeStruct((N_PROMPT, D_MODEL), F32),
                   jax.ShapeDtypeStruct((N_SAMPLE, D_MODEL), F32)],
        scratch_shapes=[pltpu.VMEM((tile, D_FF), BF16)],
        compiler_params=pltpu.CompilerParams(
            dimension_semantics=("arbitrary",), vmem_limit_bytes=VMEM_LIMIT),
        name=name,
    )(hp, hs, *extra_args)


def _ffn(hp, hs, layer, gains, w):
    specs = [_layer((N_NORMS, D_MODEL), layer), _whole((D_MODEL, 2 * D_FF)),
             _whole((D_FF, D_MODEL))]
    return _token_call(_ffn_kernel, "ffn", FFN_TILE, hp, hs, specs,
                       (gains, w["ffn1_in"], w["ffn1_out"]))


def _ffn_ple(hp, hs, pp, ps, layer, gains, w):
    tile = FFN_PLE_TILE
    specs = [_prompt_rows(PLE_DIM, tile, layer), _sample_rows(PLE_DIM, layer),
             _layer((N_NORMS, D_MODEL), layer), _whole((D_MODEL, 2 * D_FF)),
             _whole((D_FF, D_MODEL)), _whole((D_MODEL, D_MODEL)), _whole((PLE_DIM, D_MODEL))]
    return _token_call(_ffn_ple_kernel, "ffn_ple", tile, hp, hs, specs,
                       (pp, ps, gains, w["ffn2_in"], w["ffn2_out"], w["ple_gate"], w["ple"]))


def _short_conv(cin, cw_ref, prev2, prev1, rows):
    sh1 = jnp.where(rows < 1, prev1, pltpu.roll(cin, 1, 0))
    sh2 = jnp.where(rows < 2, prev2, pltpu.roll(cin, 2, 0))
    return sh2 * cw_ref[0:1, :] + sh1 * cw_ref[1:2, :] + cin * cw_ref[2:3, :]


def _merge_out(x, ua, ub, gates, wo_ref, g_ref):
    merged = jax.nn.sigmoid(gates[:, :D_MODEL]) * ua + jax.nn.sigmoid(gates[:, D_MODEL:]) * ub
    mix = _dot(merged.astype(BF16), wo_ref[...])
    return x + _rms(mix, g_ref, 3)


def _mixer_weight_specs(layer):
    return [_layer((N_NORMS, D_MODEL), layer), _whole((D_MODEL, N_IN)),
            _layer((1, D_RET), layer), _whole((D_RET, D_MODEL)),
            _layer((CONV_W, CONV_CH), layer), _whole((CONV_CH, D_MODEL)),
            _whole((D_MODEL, D_MODEL))]


def _mixer_weights(small, w):
    norm_g, ret_gn, conv_w = small
    return (norm_g, w["in"], ret_gn, w["ret_out"], conv_w, w["conv_out"], w["o"])


def _mixp_kernel(x_ref, g_ref, win_ref, gn_ref, wret_ref, cw_ref, wconv_ref, wo_ref,
                 cos_ref, sin_ref, dmask_ref, qdec_ref, kdec_ref, cdec_ref, *rest, n_cast):
    cast_src = rest[:n_cast]
    o_ref, sret_ref, sconv_ref = rest[n_cast:n_cast + 3]
    cast_dst = rest[n_cast + 3:2 * n_cast + 3]
    ya_ref = rest[-1]
    t = pl.program_id(1)
    step = pl.program_id(0) * pl.num_programs(1) + t

    @pl.when(t == 0)
    def _():
        sret_ref[...] = jnp.zeros_like(sret_ref)
        sconv_ref[...] = jnp.zeros_like(sconv_ref)

    for (_, _, _, blocks), src, dst in zip(CAST_PLAN, cast_src, cast_dst):
        if blocks == MIXP_STEPS:
            dst[...] = src[...].astype(BF16)
        else:
            @pl.when(step < blocks)
            def _(src=src, dst=dst):
                dst[...] = src[...].astype(BF16)

    x = x_ref[...]
    xn = _rms(x, g_ref, 2).astype(BF16)
    cos = cos_ref[...]
    sin = sin_ref[...]

    qk = _dot(xn, win_ref[:, OFF_QK:OFF_VG])
    vg = _dot(xn, win_ref[:, OFF_VG:OFF_CONV])
    cv = _dot(xn, win_ref[:, OFF_CONV:OFF_GATE])
    gates = _dot(xn, win_ref[:, OFF_GATE:])

    cin = cv[:, CONV_CH:2 * CONV_CH] * cv[:, 2 * CONV_CH:]
    rows = lax.broadcasted_iota(jnp.int32, (TOKEN_TILE, CONV_CH), 0)
    prev1 = sconv_ref[7:8, :]
    prev2 = jnp.where(rows < 1, sconv_ref[6:7, :], prev1)
    conv = _short_conv(cin, cw_ref, prev2, prev1, rows)
    sconv_ref[...] = cin[TOKEN_TILE - 8:, :]
    ub = _dot((cv[:, :CONV_CH] * conv).astype(BF16), wconv_ref[...])

    for h in range(HEADS):
        hs = slice(h * HEAD_DIM, (h + 1) * HEAD_DIM)
        ks = slice(D_RET + h * HEAD_DIM, D_RET + (h + 1) * HEAD_DIM)
        qh = _rotary(qk[:, hs], cos, sin)
        kh = _rotary(qk[:, ks], cos, sin) * K_SCALE
        state = sret_ref[h]
        for c in range(TOKEN_TILE // RET_CHUNK):
            rs = slice(c * RET_CHUNK, (c + 1) * RET_CHUNK)
            qc = qh[rs].astype(BF16)
            kc = kh[rs]
            vc = vg[rs, hs].astype(BF16)
            scores = _dot_nt(qc, kc.astype(BF16)) * dmask_ref[h]
            o = _dot(scores.astype(BF16), vc) + _dot(qc, state.astype(BF16)) * qdec_ref[h]
            kd = (kc * kdec_ref[h]).astype(BF16)
            state = cdec_ref[h][0:1, :] * state + _dot_tn(kd, vc)
            on = _group_norm(o, gn_ref[:, hs])
            ya_ref[rs, hs] = (_silu(vg[rs, ks]) * on).astype(BF16)
        sret_ref[h] = state
    ua = _dot(ya_ref[...], wret_ref[...])

    for rb in _tail_blocks(TOKEN_TILE):
        o_ref[rb, :] = _merge_out(x[rb], ua[rb], ub[rb], gates[rb], wo_ref, g_ref)


def _mix_prompt(hp, layer, weights, tables, next_f32):
    nt = SEQ // TOKEN_TILE
    row = pl.BlockSpec((TOKEN_TILE, D_MODEL), lambda b, t: (b * nt + t, 0))
    pos = pl.BlockSpec((TOKEN_TILE, HEAD_DIM), lambda b, t: (t, 0))
    cast_in, cast_out, cast_shapes = [], [], []
    for (_, rows, cols, blocks), _ in zip(CAST_PLAN, next_f32):
        last = blocks - 1
        cast_in.append(pl.BlockSpec(
            (None, rows // blocks, cols),
            lambda b, t, last=last: (layer + 1, jnp.minimum(b * nt + t, last), 0)))
        cast_out.append(pl.BlockSpec(
            (rows // blocks, cols), lambda b, t, last=last: (jnp.minimum(b * nt + t, last), 0)))
        cast_shapes.append(jax.ShapeDtypeStruct((rows, cols), BF16))
    return pl.pallas_call(
        functools.partial(_mixp_kernel, n_cast=len(next_f32)),
        grid=(BATCH, nt),
        in_specs=[row] + _mixer_weight_specs(layer) + [
            pos, pos,
            _whole((HEADS, RET_CHUNK, RET_CHUNK)), _whole((HEADS, RET_CHUNK, HEAD_DIM)),
            _whole((HEADS, RET_CHUNK, HEAD_DIM)), _whole((HEADS, 8, HEAD_DIM))] + cast_in,
        out_specs=[row,
                   pl.BlockSpec((None, HEADS, HEAD_DIM, HEAD_DIM), lambda b, t: (b, 0, 0, 0)),
                   pl.BlockSpec((None, 8, CONV_CH), lambda b, t: (b, 0, 0))] + cast_out,
        out_shape=[jax.ShapeDtypeStruct((N_PROMPT, D_MODEL), F32),
                   jax.ShapeDtypeStruct((BATCH, HEADS, HEAD_DIM, HEAD_DIM), F32),
                   jax.ShapeDtypeStruct((BATCH, 8, CONV_CH), F32)] + cast_shapes,
        scratch_shapes=[pltpu.VMEM((TOKEN_TILE, D_RET), BF16)],
        compiler_params=pltpu.CompilerParams(
            dimension_semantics=("arbitrary", "arbitrary"), vmem_limit_bytes=VMEM_LIMIT),
        name="mix_prompt",
    )(hp, *weights, *tables, *next_f32)


def _mixs_kernel(x_ref, g_ref, win_ref, gn_ref, wret_ref, cw_ref, wconv_ref, wo_ref,
                 cos_ref, sin_ref, bdm_ref, qdec_ref, kdec_ref, cdec_ref, scx_ref, st_ref,
                 *rest):
    (o_ref, sret_ref, ctail_ref,
     q_ref, kd_ref, v_ref, oa_ref, g_scr, ub_ref, gates_ref) = rest[-10:]
    i = pl.program_id(0)
    group_rows = SAMPLE_GROUP * DEC_SEQ

    @pl.when(i == 0)
    def _prologue():
        x = x_ref[...]
        xn = _rms(x, g_ref, 2).astype(BF16)
        cos = cos_ref[...]
        sin = sin_ref[...]
        qk = _dot(xn, win_ref[:, OFF_QK:OFF_VG])
        vg = _dot(xn, win_ref[:, OFF_VG:OFF_CONV])
        v_ref[...] = vg[:, :D_RET]
        g_scr[...] = vg[:, D_RET:]
        for h in range(HEADS):
            hs = slice(h * HEAD_DIM, (h + 1) * HEAD_DIM)
            ks = slice(D_RET + h * HEAD_DIM, D_RET + (h + 1) * HEAD_DIM)
            qh = _rotary(qk[:, hs], cos, sin)
            kh = _rotary(qk[:, ks], cos, sin) * K_SCALE
            q_ref[:, hs] = qh
            for c in range(N_SAMPLE // SAMPLE_BLOCK):
                rs = slice(c * SAMPLE_BLOCK, (c + 1) * SAMPLE_BLOCK)
                kc = kh[rs]
                scores = _dot_nt(qh[rs].astype(BF16), kc.astype(BF16)) * bdm_ref[h]
                oa_ref[rs, hs] = _dot(scores.astype(BF16), vg[rs, hs].astype(BF16))
                kd_ref[rs, hs] = kc * kdec_ref[h]
        cv = _dot(xn, win_ref[:, OFF_CONV:OFF_GATE])
        cin = cv[:, CONV_CH:2 * CONV_CH] * cv[:, 2 * CONV_CH:]
        ctail_ref[...] = cin
        scx = scx_ref[...]
        tpos = lax.broadcasted_iota(jnp.int32, (N_SAMPLE, CONV_CH), 0) & (DEC_SEQ - 1)
        prev1 = pltpu.roll(scx, N_SAMPLE - 1, 0)
        conv = _short_conv(cin, cw_ref, scx, prev1, tpos)
        ub_ref[...] = _dot((cv[:, :CONV_CH] * conv).astype(BF16), wconv_ref[...])
        gates_ref[...] = _dot(xn, win_ref[:, OFF_GATE:])

    r0 = pl.multiple_of(i * group_rows, group_rows)
    first = lax.broadcasted_iota(jnp.int32, (8, HEAD_DIM), 0) < DEC_SEQ
    for h in range(HEADS):
        hs = slice(h * HEAD_DIM, (h + 1) * HEAD_DIM)
        cdec = cdec_ref[h][0:1, :]
        for j in range(SAMPLE_GROUP // 2):
            rt = pl.ds(pl.multiple_of(r0 + j * 8, 8), 8)
            qt = q_ref[rt, hs].astype(BF16)
            kt = kd_ref[rt, hs]
            vt = v_ref[rt, hs].astype(BF16)
            sa = st_ref[2 * j, h]
            sb = st_ref[2 * j + 1, h]
            cross = jnp.where(first, _dot(qt, sa.astype(BF16)), _dot(qt, sb.astype(BF16)))
            oa_ref[rt, hs] = oa_ref[rt, hs] + cross * qdec_ref[h]
            ka = jnp.where(first, kt, 0.0).astype(BF16)
            kb = jnp.where(first, 0.0, kt).astype(BF16)
            sret_ref[2 * j, h] = cdec * sa + _dot_tn(ka, vt)
            sret_ref[2 * j + 1, h] = cdec * sb + _dot_tn(kb, vt)

    @pl.when(i == pl.num_programs(0) - 1)
    def _epilogue():
        for h in range(HEADS):
            hs = slice(h * HEAD_DIM, (h + 1) * HEAD_DIM)
            on = _group_norm(oa_ref[:, hs], gn_ref[:, hs])
            g_scr[:, hs] = _silu(g_scr[:, hs]) * on
        ua = _dot(g_scr[...].astype(BF16), wret_ref[...])
        for rb in _tail_blocks(N_SAMPLE):
            o_ref[rb, :] = _merge_out(x_ref[rb, :], ua[rb], ub_ref[rb, :], gates_ref[rb, :],
                                      wo_ref, g_ref)


def _mix_sample(hs, layer, weights, tables, scx, state_ret, ret_acc):
    row = pl.BlockSpec((N_SAMPLE, D_MODEL), lambda i: (0, 0))
    st_block = (None, SAMPLE_GROUP, HEADS, HEAD_DIM, HEAD_DIM)
    st_spec = pl.BlockSpec(st_block, lambda i: (layer, i, 0, 0, 0))
    in_specs = [row] + _mixer_weight_specs(layer) + [
        _whole((N_SAMPLE, HEAD_DIM)), _whole((N_SAMPLE, HEAD_DIM)),
        _whole((HEADS, SAMPLE_BLOCK, SAMPLE_BLOCK)), _whole((HEADS, 8, HEAD_DIM)),
        _whole((HEADS, SAMPLE_BLOCK, HEAD_DIM)), _whole((HEADS, 8, HEAD_DIM)),
        _layer((N_SAMPLE, CONV_CH), layer), st_spec]
    args = [hs, *weights, *tables, scx, state_ret]
    aliases = {}
    if ret_acc is not None:
        aliases = {len(args): 1}
        in_specs.append(pl.BlockSpec(memory_space=pl.ANY))
        args.append(ret_acc)
    return pl.pallas_call(
        _mixs_kernel,
        grid=(DEC_BATCH // SAMPLE_GROUP,),
        in_specs=in_specs,
        out_specs=[row, st_spec, pl.BlockSpec((N_SAMPLE, CONV_CH), lambda i: (0, 0))],
        out_shape=[jax.ShapeDtypeStruct((N_SAMPLE, D_MODEL), F32),
                   jax.ShapeDtypeStruct((DEPTH, DEC_BATCH, HEADS, HEAD_DIM, HEAD_DIM), F32),
                   jax.ShapeDtypeStruct((N_SAMPLE, CONV_CH), F32)],
        scratch_shapes=[pltpu.VMEM((N_SAMPLE, D_RET), F32),
                        pltpu.VMEM((N_SAMPLE, D_RET), F32),
                        pltpu.VMEM((N_SAMPLE, D_RET), F32),
                        pltpu.VMEM((N_SAMPLE, D_RET), F32),
                        pltpu.VMEM((N_SAMPLE, D_RET), F32),
                        pltpu.VMEM((N_SAMPLE, D_MODEL), F32),
                        pltpu.VMEM((N_SAMPLE, 2 * D_MODEL), F32)],
        input_output_aliases=aliases,
        compiler_params=pltpu.CompilerParams(
            dimension_semantics=("arbitrary",), vmem_limit_bytes=VMEM_LIMIT),
        name="mix_sample",
    )(*args)


def _rope_tables(pos):
    inv_freq = ROPE_BASE ** (-jnp.arange(0, HEAD_DIM, 2, dtype=F32) / HEAD_DIM)
    ang = pos[:, None] * inv_freq[None, :]
    cos = jnp.cos(ang)
    sin = jnp.sin(ang)
    return jnp.concatenate([cos, cos], -1), jnp.concatenate([-sin, sin], -1)


def _decay_tables(chunk):
    log_gamma = jnp.log1p(-jnp.exp2(-5.0 - jnp.arange(HEADS, dtype=F32)))
    idx = jnp.arange(chunk, dtype=F32)
    diff = idx[:, None] - idx[None, :]
    dmask = jnp.where(diff[None] >= 0,
                      jnp.exp(log_gamma[:, None, None] * jnp.maximum(diff, 0.0)[None]), 0.0)
    qdec = jnp.exp(log_gamma[:, None] * (idx[None] + 1.0))
    kdec = jnp.exp(log_gamma[:, None] * (chunk - 1.0 - idx[None]))
    cdec = jnp.exp(log_gamma * chunk)
    return dmask, qdec, kdec, cdec


def _lanes(t, reps):
    t = jnp.tile(t, (1, reps))
    return jnp.broadcast_to(t[:, :, None], t.shape + (HEAD_DIM,))


def _prompt_tables():
    cos, sin = _rope_tables(jnp.arange(SEQ, dtype=F32))
    dmask, qdec, kdec, cdec = _decay_tables(RET_CHUNK)
    cdec = jnp.broadcast_to(cdec[:, None, None], (HEADS, 8, HEAD_DIM))
    return cos, sin, dmask, _lanes(qdec, 1), _lanes(kdec, 1), cdec


def _sample_tables():
    cos, sin = _rope_tables(PAST_LEN + jnp.arange(DEC_SEQ, dtype=F32))
    cos = jnp.tile(cos, (DEC_BATCH, 1))
    sin = jnp.tile(sin, (DEC_BATCH, 1))
    dmask, qdec, kdec, cdec = _decay_tables(DEC_SEQ)
    seqs_per_block = SAMPLE_BLOCK // DEC_SEQ
    eye = jnp.eye(seqs_per_block, dtype=F32)
    bdm = jnp.einsum("ab,hij->haibj", eye, dmask).reshape(HEADS, SAMPLE_BLOCK, SAMPLE_BLOCK)
    cdec = jnp.broadcast_to(cdec[:, None, None], (HEADS, 8, HEAD_DIM))
    return cos, sin, bdm, _lanes(qdec, 8 // DEC_SEQ), _lanes(kdec, seqs_per_block), cdec


def kernel(x_prompt, x_sample, state_ret, state_conv, p_prompt, p_sample, norm_g, w_ffn1_in,
           w_ffn1_out, w_in, ret_gn, w_ret_out, conv_w, w_conv_out, w_o, w_ffn2_in, w_ffn2_out,
           w_ple_gate, w_ple):
    hp = x_prompt.reshape(N_PROMPT, D_MODEL)
    hs = x_sample.reshape(N_SAMPLE, D_MODEL)
    pp = p_prompt.reshape(DEPTH, N_PROMPT, PLE_DIM)
    ps = p_sample.reshape(DEPTH, N_SAMPLE, PLE_DIM)
    tables_p = _prompt_tables()
    tables_s = _sample_tables()

    f32_weights = (w_ffn1_in, w_ffn1_out, w_in, w_ret_out, w_conv_out, w_o, w_ffn2_in,
                   w_ffn2_out, w_ple_gate, w_ple)
    names = [name for name, _, _, _ in CAST_PLAN]
    w = {name: wf[0].astype(BF16) for name, wf in zip(names, f32_weights)}
    small = (norm_g, ret_gn.reshape(DEPTH, 1, D_RET), conv_w)
    scx = jnp.concatenate([state_conv, jnp.zeros_like(state_conv)], axis=2)
    scx = scx.reshape(DEPTH, N_SAMPLE, CONV_CH)

    ret_p, conv_p, conv_s = [], [], []
    ret_s = None
    for i in range(DEPTH):
        hp, hs = _ffn(hp, hs, i, norm_g, w)
        mixer_w = _mixer_weights(small, w)
        hp, rp, cp, *w_next = _mix_prompt(hp, i, mixer_w, tables_p,
                                          f32_weights if i + 1 < DEPTH else ())
        hs, ret_s, ct = _mix_sample(hs, i, mixer_w, tables_s, scx, state_ret, ret_s)
        hp, hs = _ffn_ple(hp, hs, pp, ps, i, norm_g, w)
        if w_next:
            w = dict(zip(names, w_next))
        ret_p.append(rp)
        conv_p.append(cp[:, 8 - (CONV_W - 1):, :])
        conv_s.append(ct.reshape(DEC_BATCH, DEC_SEQ, CONV_CH)[:, DEC_SEQ - (CONV_W - 1):, :])

    return (hp.reshape(BATCH, SEQ, D_MODEL), hs.reshape(DEC_BATCH, DEC_SEQ, D_MODEL),
            jnp.stack(ret_p), jnp.stack(conv_p), ret_s, jnp.stack(conv_s))
```

```python
import functools

import jax
import jax.numpy as jnp
from jax import lax
from jax.experimental import pallas as pl
from jax.experimental.pallas import tpu as pltpu

F32 = jnp.float32
BF16 = jnp.bfloat16

D_MODEL = 1024
BATCH = 8
SEQ = 2048
DEPTH = 4
DEC_BATCH = 128
DEC_SEQ = 4
PAST_LEN = 16384
HEADS = 4
HEAD_DIM = 128
D_RET = HEADS * HEAD_DIM
CONV_CH = 512
CONV_W = 3
D_FF = 2816
PLE_DIM = 256
N_NORMS = 8
ROPE_BASE = 10000.0
EPS = 1e-6
K_SCALE = HEAD_DIM ** -0.5

N_PROMPT = BATCH * SEQ
N_SAMPLE = DEC_BATCH * DEC_SEQ

OFF_QK = 0
OFF_VG = 2 * D_RET
OFF_CONV = 4 * D_RET
OFF_GATE = 4 * D_RET + 3 * CONV_CH
N_IN = OFF_GATE + 2 * D_MODEL

TOKEN_TILE = N_SAMPLE
FFN_TILE = 1024
FFN_PLE_TILE = 1024
TAIL_SPLIT = 2
RET_CHUNK = 256
FF_CHUNK = 256
N_FF_CHUNKS = D_FF // FF_CHUNK
SAMPLE_GROUP = 16
SAMPLE_BLOCK = 128
VMEM_LIMIT = 56 * 1024 * 1024
MIXP_STEPS = N_PROMPT // TOKEN_TILE

CAST_PLAN = (
    ("ffn1_in", D_MODEL, 2 * D_FF, 32),
    ("ffn1_out", D_FF, D_MODEL, 16),
    ("in", D_MODEL, N_IN, 32),
    ("ret_out", D_RET, D_MODEL, 32),
    ("conv_out", CONV_CH, D_MODEL, 32),
    ("o", D_MODEL, D_MODEL, 32),
    ("ffn2_in", D_MODEL, 2 * D_FF, 32),
    ("ffn2_out", D_FF, D_MODEL, 16),
    ("ple_gate", D_MODEL, D_MODEL, 32),
    ("ple", PLE_DIM, D_MODEL, 16),
)


def _dot(a, b):
    return jnp.dot(a, b, preferred_element_type=F32)


def _dot_nt(a, b):
    return lax.dot_general(a, b, (((1,), (1,)), ((), ())), preferred_element_type=F32)


def _dot_tn(a, b):
    return lax.dot_general(a, b, (((0,), (0,)), ((), ())), preferred_element_type=F32)


def _rms(x, g_ref, j):
    ms = jnp.mean(x * x, axis=-1, keepdims=True)
    return x * lax.rsqrt(ms + EPS) * g_ref[j:j + 1, :]


def _silu(x):
    return x * jax.nn.sigmoid(x)


def _rotary(t, cos, sin_signed):
    return t * cos + pltpu.roll(t, HEAD_DIM // 2, 1) * sin_signed


def _group_norm(o, gn):
    mu = jnp.mean(o, axis=-1, keepdims=True)
    d = o - mu
    var = jnp.mean(d * d, axis=-1, keepdims=True)
    return d * lax.rsqrt(var + EPS) * gn


def _tail_blocks(rows):
    step = rows // TAIL_SPLIT
    return [slice(r * step, (r + 1) * step) for r in range(TAIL_SPLIT)]


def _whole(shape):
    zeros = (0,) * len(shape)
    return pl.BlockSpec(shape, lambda *_: zeros, pipeline_mode=pl.Buffered(1))


def _layer(shape, layer):
    index = (layer,) + (0,) * len(shape)
    return pl.BlockSpec((None,) + tuple(shape), lambda *_: index, pipeline_mode=pl.Buffered(1))


def _prompt_rows(width, tile, layer=None):
    last = N_PROMPT // tile - 1
    if layer is None:
        return pl.BlockSpec((tile, width), lambda i: (jnp.minimum(i, last), 0))
    return pl.BlockSpec((None, tile, width), lambda i: (layer, jnp.minimum(i, last), 0))


def _sample_rows(width, layer=None):
    if layer is None:
        return pl.BlockSpec((N_SAMPLE, width), lambda i: (0, 0))
    return pl.BlockSpec((None, N_SAMPLE, width), lambda i: (layer, 0, 0))


def _ffn_rows(x_ref, p_ref, o_ref, g_ref, wi_ref, wo_ref, wg_ref, we_ref, act_ref, first_norm):
    x = x_ref[...]
    n = x.shape[0]
    xn = _rms(x, g_ref, first_norm).astype(BF16)
    for c in range(N_FF_CHUNKS):
        cs = slice(c * FF_CHUNK, (c + 1) * FF_CHUNK)
        us = slice(D_FF + c * FF_CHUNK, D_FF + (c + 1) * FF_CHUNK)
        act = _silu(_dot(xn, wi_ref[:, cs])) * _dot(xn, wi_ref[:, us])
        act_ref[0:n, cs] = act.astype(BF16)
    for rows in _tail_blocks(n):
        y = _dot(act_ref[rows, :], wo_ref[...])
        xr = x[rows] + 0.5 * _rms(y, g_ref, first_norm + 1)
        if p_ref is not None:
            gate = jax.nn.sigmoid(_dot(_rms(xr, g_ref, 6).astype(BF16), wg_ref[...]))
            emb = _dot(p_ref[rows, :].astype(BF16), we_ref[...])
            xr = xr + _rms(emb * gate, g_ref, 7)
        o_ref[rows, :] = xr


def _ffn_kernel(xp_ref, xs_ref, g_ref, wi_ref, wo_ref, op_ref, os_ref, act_ref):
    i = pl.program_id(0)
    sample_step = pl.num_programs(0) - 1

    @pl.when(i < sample_step)
    def _():
        _ffn_rows(xp_ref, None, op_ref, g_ref, wi_ref, wo_ref, None, None, act_ref, 0)

    @pl.when(i == sample_step)
    def _():
        _ffn_rows(xs_ref, None, os_ref, g_ref, wi_ref, wo_ref, None, None, act_ref, 0)


def _ffn_ple_kernel(xp_ref, xs_ref, pp_ref, ps_ref, g_ref, wi_ref, wo_ref, wg_ref, we_ref,
                    op_ref, os_ref, act_ref):
    i = pl.program_id(0)
    sample_step = pl.num_programs(0) - 1

    @pl.when(i < sample_step)
    def _():
        _ffn_rows(xp_ref, pp_ref, op_ref, g_ref, wi_ref, wo_ref, wg_ref, we_ref, act_ref, 4)

    @pl.when(i == sample_step)
    def _():
        _ffn_rows(xs_ref, ps_ref, os_ref, g_ref, wi_ref, wo_ref, wg_ref, we_ref, act_ref, 4)


def _token_call(body, name, tile, hp, hs, extra_specs, extra_args):
    return pl.pallas_call(
        body,
        grid=(N_PROMPT // tile + 1,),
        in_specs=[_prompt_rows(D_MODEL, tile), _sample_rows(D_MODEL)] + extra_specs,
        out_specs=[_prompt_rows(D_MODEL, tile), _sample_rows(D_MODEL)],
        out_shape=[jax.ShapeDtypeStruct((N_PROMPT, D_MODEL), F32),
                   jax.ShapeDtypeStruct((N_SAMPLE, D_MODEL), F32)],
        scratch_shapes=[pltpu.VMEM((tile, D_FF), BF16)],
        compiler_params=pltpu.CompilerParams(
            dimension_semantics=("arbitrary",), vmem_limit_bytes=VMEM_LIMIT),
        name=name,
    )(hp, hs, *extra_args)


def _ffn(hp, hs, layer, gains, w):
    specs = [_layer((N_NORMS, D_MODEL), layer), _whole((D_MODEL, 2 * D_FF)),
             _whole((D_FF, D_MODEL))]
    return _token_call(_ffn_kernel, "ffn", FFN_TILE, hp, hs, specs,
                       (gains, w["ffn1_in"], w["ffn1_out"]))


def _ffn_ple(hp, hs, pp, ps, layer, gains, w):
    tile = FFN_PLE_TILE
    specs = [_prompt_rows(PLE_DIM, tile, layer), _sample_rows(PLE_DIM, layer),
             _layer((N_NORMS, D_MODEL), layer), _whole((D_MODEL, 2 * D_FF)),
             _whole((D_FF, D_MODEL)), _whole((D_MODEL, D_MODEL)), _whole((PLE_DIM, D_MODEL))]
    return _token_call(_ffn_ple_kernel, "ffn_ple", tile, hp, hs, specs,
                       (pp, ps, gains, w["ffn2_in"], w["ffn2_out"], w["ple_gate"], w["ple"]))


def _short_conv(cin, cw_ref, prev2, prev1, rows):
    sh1 = jnp.where(rows < 1, prev1, pltpu.roll(cin, 1, 0))
    sh2 = jnp.where(rows < 2, prev2, pltpu.roll(cin, 2, 0))
    return sh2 * cw_ref[0:1, :] + sh1 * cw_ref[1:2, :] + cin * cw_ref[2:3, :]


def _merge_out(x, ua, ub, gates, wo_ref, g_ref):
    merged = jax.nn.sigmoid(gates[:, :D_MODEL]) * ua + jax.nn.sigmoid(gates[:, D_MODEL:]) * ub
    mix = _dot(merged.astype(BF16), wo_ref[...])
    return x + _rms(mix, g_ref, 3)


def _mixer_weight_specs(layer):
    return [_layer((N_NORMS, D_MODEL), layer), _whole((D_MODEL, N_IN)),
            _layer((1, D_RET), layer), _whole((D_RET, D_MODEL)),
            _layer((CONV_W, CONV_CH), layer), _whole((CONV_CH, D_MODEL)),
            _whole((D_MODEL, D_MODEL))]


def _mixer_weights(small, w):
    norm_g, ret_gn, conv_w = small
    return (norm_g, w["in"], ret_gn, w["ret_out"], conv_w, w["conv_out"], w["o"])


def _mixp_kernel(x_ref, g_ref, win_ref, gn_ref, wret_ref, cw_ref, wconv_ref, wo_ref,
                 cos_ref, sin_ref, dmask_ref, qdec_ref, kdec_ref, cdec_ref, *rest, n_cast):
    cast_src = rest[:n_cast]
    o_ref, sret_ref, sconv_ref = rest[n_cast:n_cast + 3]
    cast_dst = rest[n_cast + 3:2 * n_cast + 3]
    ya_ref = rest[-1]
    t = pl.program_id(1)
    step = pl.program_id(0) * pl.num_programs(1) + t

    @pl.when(t == 0)
    def _():
        sret_ref[...] = jnp.zeros_like(sret_ref)
        sconv_ref[...] = jnp.zeros_like(sconv_ref)

    for (_, _, _, blocks), src, dst in zip(CAST_PLAN, cast_src, cast_dst):
        if blocks == MIXP_STEPS:
            dst[...] = src[...].astype(BF16)
        else:
            @pl.when(step < blocks)
            def _(src=src, dst=dst):
                dst[...] = src[...].astype(BF16)

    x = x_ref[...]
    xn = _rms(x, g_ref, 2).astype(BF16)
    cos = cos_ref[...]
    sin = sin_ref[...]

    qk = _dot(xn, win_ref[:, OFF_QK:OFF_VG])
    vg = _dot(xn, win_ref[:, OFF_VG:OFF_CONV])
    cv = _dot(xn, win_ref[:, OFF_CONV:OFF_GATE])
    gates = _dot(xn, win_ref[:, OFF_GATE:])

    cin = cv[:, CONV_CH:2 * CONV_CH] * cv[:, 2 * CONV_CH:]
    rows = lax.broadcasted_iota(jnp.int32, (TOKEN_TILE, CONV_CH), 0)
    prev1 = sconv_ref[7:8, :]
    prev2 = jnp.where(rows < 1, sconv_ref[6:7, :], prev1)
    conv = _short_conv(cin, cw_ref, prev2, prev1, rows)
    sconv_ref[...] = cin[TOKEN_TILE - 8:, :]
    ub = _dot((cv[:, :CONV_CH] * conv).astype(BF16), wconv_ref[...])

    for h in range(HEADS):
        hs = slice(h * HEAD_DIM, (h + 1) * HEAD_DIM)
        ks = slice(D_RET + h * HEAD_DIM, D_RET + (h + 1) * HEAD_DIM)
        qh = _rotary(qk[:, hs], cos, sin)
        kh = _rotary(qk[:, ks], cos, sin) * K_SCALE
        state = sret_ref[h]
        for c in range(TOKEN_TILE // RET_CHUNK):
            rs = slice(c * RET_CHUNK, (c + 1) * RET_CHUNK)
            qc = qh[rs].astype(BF16)
            kc = kh[rs]
            vc = vg[rs, hs].astype(BF16)
            scores = _dot_nt(qc, kc.astype(BF16)) * dmask_ref[h]
            o = _dot(scores.astype(BF16), vc) + _dot(qc, state.astype(BF16)) * qdec_ref[h]
            kd = (kc * kdec_ref[h]).astype(BF16)
            state = cdec_ref[h][0:1, :] * state + _dot_tn(kd, vc)
            on = _group_norm(o, gn_ref[:, hs])
            ya_ref[rs, hs] = (_silu(vg[rs, ks]) * on).astype(BF16)
        sret_ref[h] = state
    ua = _dot(ya_ref[...], wret_ref[...])

    for rb in _tail_blocks(TOKEN_TILE):
        o_ref[rb, :] = _merge_out(x[rb], ua[rb], ub[rb], gates[rb], wo_ref, g_ref)


def _mix_prompt(hp, layer, weights, tables, next_f32):
    nt = SEQ // TOKEN_TILE
    row = pl.BlockSpec((TOKEN_TILE, D_MODEL), lambda b, t: (b * nt + t, 0))
    pos = pl.BlockSpec((TOKEN_TILE, HEAD_DIM), lambda b, t: (t, 0))
    cast_in, cast_out, cast_shapes = [], [], []
    for (_, rows, cols, blocks), _ in zip(CAST_PLAN, next_f32):
        last = blocks - 1
        cast_in.append(pl.BlockSpec(
            (None, rows // blocks, cols),
            lambda b, t, last=last: (layer + 1, jnp.minimum(b * nt + t, last), 0)))
        cast_out.append(pl.BlockSpec(
            (rows // blocks, cols), lambda b, t, last=last: (jnp.minimum(b * nt + t, last), 0)))
        cast_shapes.append(jax.ShapeDtypeStruct((rows, cols), BF16))
    return pl.pallas_call(
        functools.partial(_mixp_kernel, n_cast=len(next_f32)),
        grid=(BATCH, nt),
        in_specs=[row] + _mixer_weight_specs(layer) + [
            pos, pos,
            _whole((HEADS, RET_CHUNK, RET_CHUNK)), _whole((HEADS, RET_CHUNK, HEAD_DIM)),
            _whole((HEADS, RET_CHUNK, HEAD_DIM)), _whole((HEADS, 8, HEAD_DIM))] + cast_in,
        out_specs=[row,
                   pl.BlockSpec((None, HEADS, HEAD_DIM, HEAD_DIM), lambda b, t: (b, 0, 0, 0)),
                   pl.BlockSpec((None, 8, CONV_CH), lambda b, t: (b, 0, 0))] + cast_out,
        out_shape=[jax.ShapeDtypeStruct((N_PROMPT, D_MODEL), F32),
                   jax.ShapeDtypeStruct((BATCH, HEADS, HEAD_DIM, HEAD_DIM), F32),
                   jax.ShapeDtypeStruct((BATCH, 8, CONV_CH), F32)] + cast_shapes,
        scratch_shapes=[pltpu.VMEM((TOKEN_TILE, D_RET), BF16)],
        compiler_params=pltpu.CompilerParams(
            dimension_semantics=("arbitrary", "arbitrary"), vmem_limit_bytes=VMEM_LIMIT),
        name="mix_prompt",
    )(hp, *weights, *tables, *next_f32)


def _mixs_kernel(x_ref, g_ref, win_ref, gn_ref, wret_ref, cw_ref, wconv_ref, wo_ref,
                 cos_ref, sin_ref, bdm_ref, qdec_ref, kdec_ref, cdec_ref, scx_ref, st_ref,
                 *rest):
    (o_ref, sret_ref, ctail_ref,
     q_ref, kd_ref, v_ref, oa_ref, g_scr, ub_ref, gates_ref) = rest[-10:]
    i = pl.program_id(0)
    group_rows = SAMPLE_GROUP * DEC_SEQ

    @pl.when(i == 0)
    def _prologue():
        x = x_ref[...]
        xn = _rms(x, g_ref, 2).astype(BF16)
        cos = cos_ref[...]
        sin = sin_ref[...]
        qk = _dot(xn, win_ref[:, OFF_QK:OFF_VG])
        vg = _dot(xn, win_ref[:, OFF_VG:OFF_CONV])
        v_ref[...] = vg[:, :D_RET]
        g_scr[...] = vg[:, D_RET:]
        for h in range(HEADS):
            hs = slice(h * HEAD_DIM, (h + 1) * HEAD_DIM)
            ks = slice(D_RET + h * HEAD_DIM, D_RET + (h + 1) * HEAD_DIM)
            qh = _rotary(qk[:, hs], cos, sin)
            kh = _rotary(qk[:, ks], cos, sin) * K_SCALE
            q_ref[:, hs] = qh
            for c in range(N_SAMPLE // SAMPLE_BLOCK):
                rs = slice(c * SAMPLE_BLOCK, (c + 1) * SAMPLE_BLOCK)
                kc = kh[rs]
                scores = _dot_nt(qh[rs].astype(BF16), kc.astype(BF16)) * bdm_ref[h]
                oa_ref[rs, hs] = _dot(scores.astype(BF16), vg[rs, hs].astype(BF16))
                kd_ref[rs, hs] = kc * kdec_ref[h]
        cv = _dot(xn, win_ref[:, OFF_CONV:OFF_GATE])
        cin = cv[:, CONV_CH:2 * CONV_CH] * cv[:, 2 * CONV_CH:]
        ctail_ref[...] = cin
        scx = scx_ref[...]
        tpos = lax.broadcasted_iota(jnp.int32, (N_SAMPLE, CONV_CH), 0) & (DEC_SEQ - 1)
        prev1 = pltpu.roll(scx, N_SAMPLE - 1, 0)
        conv = _short_conv(cin, cw_ref, scx, prev1, tpos)
        ub_ref[...] = _dot((cv[:, :CONV_CH] * conv).astype(BF16), wconv_ref[...])
        gates_ref[...] = _dot(xn, win_ref[:, OFF_GATE:])

    r0 = pl.multiple_of(i * group_rows, group_rows)
    first = lax.broadcasted_iota(jnp.int32, (8, HEAD_DIM), 0) < DEC_SEQ
    for h in range(HEADS):
        hs = slice(h * HEAD_DIM, (h + 1) * HEAD_DIM)
        cdec = cdec_ref[h][0:1, :]
        for j in range(SAMPLE_GROUP // 2):
            rt = pl.ds(pl.multiple_of(r0 + j * 8, 8), 8)
            qt = q_ref[rt, hs].astype(BF16)
            kt = kd_ref[rt, hs]
            vt = v_ref[rt, hs].astype(BF16)
            sa = st_ref[2 * j, h]
            sb = st_ref[2 * j + 1, h]
            cross = jnp.where(first, _dot(qt, sa.astype(BF16)), _dot(qt, sb.astype(BF16)))
            oa_ref[rt, hs] = oa_ref[rt, hs] + cross * qdec_ref[h]
            ka = jnp.where(first, kt, 0.0).astype(BF16)
            kb = jnp.where(first, 0.0, kt).astype(BF16)
            sret_ref[2 * j, h] = cdec * sa + _dot_tn(ka, vt)
            sret_ref[2 * j + 1, h] = cdec * sb + _dot_tn(kb, vt)

    @pl.when(i == pl.num_programs(0) - 1)
    def _epilogue():
        for h in range(HEADS):
            hs = slice(h * HEAD_DIM, (h + 1) * HEAD_DIM)
            on = _group_norm(oa_ref[:, hs], gn_ref[:, hs])
            g_scr[:, hs] = _silu(g_scr[:, hs]) * on
        ua = _dot(g_scr[...].astype(BF16), wret_ref[...])
        for rb in _tail_blocks(N_SAMPLE):
            o_ref[rb, :] = _merge_out(x_ref[rb, :], ua[rb], ub_ref[rb, :], gates_ref[rb, :],
                                      wo_ref, g_ref)


def _mix_sample(hs, layer, weights, tables, scx, state_ret, ret_acc):
    row = pl.BlockSpec((N_SAMPLE, D_MODEL), lambda i: (0, 0))
    st_block = (None, SAMPLE_GROUP, HEADS, HEAD_DIM, HEAD_DIM)
    st_spec = pl.BlockSpec(st_block, lambda i: (layer, i, 0, 0, 0))
    in_specs = [row] + _mixer_weight_specs(layer) + [
        _whole((N_SAMPLE, HEAD_DIM)), _whole((N_SAMPLE, HEAD_DIM)),
        _whole((HEADS, SAMPLE_BLOCK, SAMPLE_BLOCK)), _whole((HEADS, 8, HEAD_DIM)),
        _whole((HEADS, SAMPLE_BLOCK, HEAD_DIM)), _whole((HEADS, 8, HEAD_DIM)),
        _layer((N_SAMPLE, CONV_CH), layer), st_spec]
    in_specs.append(pl.BlockSpec(memory_space=pl.ANY))
    args = [hs, *weights, *tables, scx, state_ret, ret_acc]
    aliases = {len(args) - 1: 1}
    return pl.pallas_call(
        _mixs_kernel,
        grid=(DEC_BATCH // SAMPLE_GROUP,),
        in_specs=in_specs,
        out_specs=[row, st_spec, pl.BlockSpec((N_SAMPLE, CONV_CH), lambda i: (0, 0))],
        out_shape=[jax.ShapeDtypeStruct((N_SAMPLE, D_MODEL), F32),
                   jax.ShapeDtypeStruct((DEPTH, DEC_BATCH, HEADS, HEAD_DIM, HEAD_DIM), F32),
                   jax.ShapeDtypeStruct((N_SAMPLE, CONV_CH), F32)],
        scratch_shapes=[pltpu.VMEM((N_SAMPLE, D_RET), F32),
                        pltpu.VMEM((N_SAMPLE, D_RET), F32),
                        pltpu.VMEM((N_SAMPLE, D_RET), F32),
                        pltpu.VMEM((N_SAMPLE, D_RET), F32),
                        pltpu.VMEM((N_SAMPLE, D_RET), F32),
                        pltpu.VMEM((N_SAMPLE, D_MODEL), F32),
                        pltpu.VMEM((N_SAMPLE, 2 * D_MODEL), F32)],
        input_output_aliases=aliases,
        compiler_params=pltpu.CompilerParams(
            dimension_semantics=("arbitrary",), vmem_limit_bytes=VMEM_LIMIT),
        name="mix_sample",
    )(*args)


def _rope_tables(pos):
    inv_freq = ROPE_BASE ** (-jnp.arange(0, HEAD_DIM, 2, dtype=F32) / HEAD_DIM)
    ang = pos[:, None] * inv_freq[None, :]
    cos = jnp.cos(ang)
    sin = jnp.sin(ang)
    return jnp.concatenate([cos, cos], -1), jnp.concatenate([-sin, sin], -1)


def _decay_tables(chunk):
    log_gamma = jnp.log1p(-jnp.exp2(-5.0 - jnp.arange(HEADS, dtype=F32)))
    idx = jnp.arange(chunk, dtype=F32)
    diff = idx[:, None] - idx[None, :]
    dmask = jnp.where(diff[None] >= 0,
                      jnp.exp(log_gamma[:, None, None] * jnp.maximum(diff, 0.0)[None]), 0.0)
    qdec = jnp.exp(log_gamma[:, None] * (idx[None] + 1.0))
    kdec = jnp.exp(log_gamma[:, None] * (chunk - 1.0 - idx[None]))
    cdec = jnp.exp(log_gamma * chunk)
    return dmask, qdec, kdec, cdec


def _lanes(t, reps):
    t = jnp.tile(t, (1, reps))
    return jnp.broadcast_to(t[:, :, None], t.shape + (HEAD_DIM,))


def _prompt_tables():
    cos, sin = _rope_tables(jnp.arange(SEQ, dtype=F32))
    dmask, qdec, kdec, cdec = _decay_tables(RET_CHUNK)
    cdec = jnp.broadcast_to(cdec[:, None, None], (HEADS, 8, HEAD_DIM))
    return cos, sin, dmask, _lanes(qdec, 1), _lanes(kdec, 1), cdec


def _sample_tables():
    cos, sin = _rope_tables(PAST_LEN + jnp.arange(DEC_SEQ, dtype=F32))
    cos = jnp.tile(cos, (DEC_BATCH, 1))
    sin = jnp.tile(sin, (DEC_BATCH, 1))
    dmask, qdec, kdec, cdec = _decay_tables(DEC_SEQ)
    seqs_per_block = SAMPLE_BLOCK // DEC_SEQ
    eye = jnp.eye(seqs_per_block, dtype=F32)
    bdm = jnp.einsum("ab,hij->haibj", eye, dmask).reshape(HEADS, SAMPLE_BLOCK, SAMPLE_BLOCK)
    cdec = jnp.broadcast_to(cdec[:, None, None], (HEADS, 8, HEAD_DIM))
    return cos, sin, bdm, _lanes(qdec, 8 // DEC_SEQ), _lanes(kdec, seqs_per_block), cdec


def kernel(x_prompt, x_sample, state_ret, state_conv, p_prompt, p_sample, norm_g, w_ffn1_in,
           w_ffn1_out, w_in, ret_gn, w_ret_out, conv_w, w_conv_out, w_o, w_ffn2_in, w_ffn2_out,
           w_ple_gate, w_ple):
    hp = x_prompt.reshape(N_PROMPT, D_MODEL)
    hs = x_sample.reshape(N_SAMPLE, D_MODEL)
    pp = p_prompt.reshape(DEPTH, N_PROMPT, PLE_DIM)
    ps = p_sample.reshape(DEPTH, N_SAMPLE, PLE_DIM)
    tables_p = _prompt_tables()
    tables_s = _sample_tables()

    f32_weights = (w_ffn1_in, w_ffn1_out, w_in, w_ret_out, w_conv_out, w_o, w_ffn2_in,
                   w_ffn2_out, w_ple_gate, w_ple)
    names = [name for name, _, _, _ in CAST_PLAN]
    w = {name: wf[0].astype(BF16) for name, wf in zip(names, f32_weights)}
    small = (norm_g, ret_gn.reshape(DEPTH, 1, D_RET), conv_w)
    scx = jnp.concatenate([state_conv, jnp.zeros_like(state_conv)], axis=2)
    scx = scx.reshape(DEPTH, N_SAMPLE, CONV_CH)

    ret_p, conv_p, conv_s = [], [], []
    ret_s = jnp.zeros((DEPTH, DEC_BATCH, HEADS, HEAD_DIM, HEAD_DIM), F32)
    for i in range(DEPTH):
        hp, hs = _ffn(hp, hs, i, norm_g, w)
        mixer_w = _mixer_weights(small, w)
        hp, rp, cp, *w_next = _mix_prompt(hp, i, mixer_w, tables_p,
                                          f32_weights if i + 1 < DEPTH else ())
        hs, ret_s, ct = _mix_sample(hs, i, mixer_w, tables_s, scx, state_ret, ret_s)
        hp, hs = _ffn_ple(hp, hs, pp, ps, i, norm_g, w)
        if w_next:
            w = dict(zip(names, w_next))
        ret_p.append(rp)
        conv_p.append(cp[:, 8 - (CONV_W - 1):, :])
        conv_s.append(ct.reshape(DEC_BATCH, DEC_SEQ, CONV_CH)[:, DEC_SEQ - (CONV_W - 1):, :])

    return (hp.reshape(BATCH, SEQ, D_MODEL), hs.reshape(DEC_BATCH, DEC_SEQ, D_MODEL),
            jnp.stack(ret_p), jnp.stack(conv_p), ret_s, jnp.stack(conv_s))
```

```python
import functools

import jax
import jax.numpy as jnp
from jax import lax
from jax.experimental import pallas as pl
from jax.experimental.pallas import tpu as pltpu

F32 = jnp.float32
BF16 = jnp.bfloat16

D_MODEL = 1024
BATCH = 8
SEQ = 2048
DEPTH = 4
DEC_BATCH = 128
DEC_SEQ = 4
PAST_LEN = 16384
HEADS = 4
HEAD_DIM = 128
D_RET = HEADS * HEAD_DIM
CONV_CH = 512
CONV_W = 3
D_FF = 2816
PLE_DIM = 256
N_NORMS = 8
ROPE_BASE = 10000.0
EPS = 1e-6
K_SCALE = HEAD_DIM ** -0.5

N_PROMPT = BATCH * SEQ
N_SAMPLE = DEC_BATCH * DEC_SEQ

OFF_QK = 0
OFF_VG = 2 * D_RET
OFF_CONV = 4 * D_RET
OFF_GATE = 4 * D_RET + 3 * CONV_CH
N_IN = OFF_GATE + 2 * D_MODEL

TOKEN_TILE = N_SAMPLE
FFN_TILE = 1024
FFN_PLE_TILE = 512
TAIL_SPLIT = 2
RET_CHUNK = 256
FF_CHUNK = 256
N_FF_CHUNKS = D_FF // FF_CHUNK
SAMPLE_GROUP = 16
SAMPLE_BLOCK = 128
VMEM_LIMIT = 56 * 1024 * 1024
MIXP_STEPS = N_PROMPT // TOKEN_TILE

CAST_PLAN = (
    ("ffn1_in", D_MODEL, 2 * D_FF, 32),
    ("ffn1_out", D_FF, D_MODEL, 16),
    ("in", D_MODEL, N_IN, 32),
    ("ret_out", D_RET, D_MODEL, 32),
    ("conv_out", CONV_CH, D_MODEL, 32),
    ("o", D_MODEL, D_MODEL, 32),
    ("ffn2_in", D_MODEL, 2 * D_FF, 32),
    ("ffn2_out", D_FF, D_MODEL, 16),
    ("ple_gate", D_MODEL, D_MODEL, 32),
    ("ple", PLE_DIM, D_MODEL, 16),
)


def _dot(a, b):
    return jnp.dot(a, b, preferred_element_type=F32)


def _dot_nt(a, b):
    return lax.dot_general(a, b, (((1,), (1,)), ((), ())), preferred_element_type=F32)


def _dot_tn(a, b):
    return lax.dot_general(a, b, (((0,), (0,)), ((), ())), preferred_element_type=F32)


def _rms(x, g_ref, j):
    ms = jnp.mean(x * x, axis=-1, keepdims=True)
    return x * lax.rsqrt(ms + EPS) * g_ref[j:j + 1, :]


def _silu(x):
    return x * jax.nn.sigmoid(x)


def _rotary(t, cos, sin_signed):
    return t * cos + pltpu.roll(t, HEAD_DIM // 2, 1) * sin_signed


def _group_norm(o, gn):
    mu = jnp.mean(o, axis=-1, keepdims=True)
    d = o - mu
    var = jnp.mean(d * d, axis=-1, keepdims=True)
    return d * lax.rsqrt(var + EPS) * gn


def _tail_blocks(rows):
    step = rows // TAIL_SPLIT
    return [slice(r * step, (r + 1) * step) for r in range(TAIL_SPLIT)]


def _whole(shape):
    zeros = (0,) * len(shape)
    return pl.BlockSpec(shape, lambda *_: zeros, pipeline_mode=pl.Buffered(1))


def _layer(shape, layer):
    index = (layer,) + (0,) * len(shape)
    return pl.BlockSpec((None,) + tuple(shape), lambda *_: index, pipeline_mode=pl.Buffered(1))


def _prompt_rows(width, tile, layer=None):
    last = N_PROMPT // tile - 1
    if layer is None:
        return pl.BlockSpec((tile, width), lambda i: (jnp.minimum(i, last), 0))
    return pl.BlockSpec((None, tile, width), lambda i: (layer, jnp.minimum(i, last), 0))


def _sample_rows(width, layer=None):
    if layer is None:
        return pl.BlockSpec((N_SAMPLE, width), lambda i: (0, 0))
    return pl.BlockSpec((None, N_SAMPLE, width), lambda i: (layer, 0, 0))


def _ffn_rows(x_ref, p_ref, o_ref, g_ref, wi_ref, wo_ref, wg_ref, we_ref, act_ref, first_norm):
    x = x_ref[...]
    n = x.shape[0]
    xn = _rms(x, g_ref, first_norm).astype(BF16)
    for c in range(N_FF_CHUNKS):
        cs = slice(c * FF_CHUNK, (c + 1) * FF_CHUNK)
        us = slice(D_FF + c * FF_CHUNK, D_FF + (c + 1) * FF_CHUNK)
        act = _silu(_dot(xn, wi_ref[:, cs])) * _dot(xn, wi_ref[:, us])
        act_ref[0:n, cs] = act.astype(BF16)
    for rows in _tail_blocks(n):
        y = _dot(act_ref[rows, :], wo_ref[...])
        xr = x[rows] + 0.5 * _rms(y, g_ref, first_norm + 1)
        if p_ref is not None:
            gate = jax.nn.sigmoid(_dot(_rms(xr, g_ref, 6).astype(BF16), wg_ref[...]))
            emb = _dot(p_ref[rows, :].astype(BF16), we_ref[...])
            xr = xr + _rms(emb * gate, g_ref, 7)
        o_ref[rows, :] = xr


def _ffn_kernel(xp_ref, xs_ref, g_ref, wi_ref, wo_ref, op_ref, os_ref, act_ref):
    i = pl.program_id(0)
    sample_step = pl.num_programs(0) - 1

    @pl.when(i < sample_step)
    def _():
        _ffn_rows(xp_ref, None, op_ref, g_ref, wi_ref, wo_ref, None, None, act_ref, 0)

    @pl.when(i == sample_step)
    def _():
        _ffn_rows(xs_ref, None, os_ref, g_ref, wi_ref, wo_ref, None, None, act_ref, 0)


def _ffn_ple_kernel(xp_ref, xs_ref, pp_ref, ps_ref, g_ref, wi_ref, wo_ref, wg_ref, we_ref,
                    op_ref, os_ref, act_ref):
    i = pl.program_id(0)
    sample_step = pl.num_programs(0) - 1

    @pl.when(i < sample_step)
    def _():
        _ffn_rows(xp_ref, pp_ref, op_ref, g_ref, wi_ref, wo_ref, wg_ref, we_ref, act_ref, 4)

    @pl.when(i == sample_step)
    def _():
        _ffn_rows(xs_ref, ps_ref, os_ref, g_ref, wi_ref, wo_ref, wg_ref, we_ref, act_ref, 4)


def _token_call(body, name, tile, hp, hs, extra_specs, extra_args):
    return pl.pallas_call(
        body,
        grid=(N_PROMPT // tile + 1,),
        in_specs=[_prompt_rows(D_MODEL, tile), _sample_rows(D_MODEL)] + extra_specs,
        out_specs=[_prompt_rows(D_MODEL, tile), _sample_rows(D_MODEL)],
        out_shape=[jax.ShapeDtypeStruct((N_PROMPT, D_MODEL), F32),
                   jax.ShapeDtypeStruct((N_SAMPLE, D_MODEL), F32)],
        scratch_shapes=[pltpu.VMEM((tile, D_FF), BF16)],
        compiler_params=pltpu.CompilerParams(
            dimension_semantics=("arbitrary",), vmem_limit_bytes=VMEM_LIMIT),
        name=name,
    )(hp, hs, *extra_args)


def _ffn(hp, hs, layer, gains, w):
    specs = [_layer((N_NORMS, D_MODEL), layer), _whole((D_MODEL, 2 * D_FF)),
             _whole((D_FF, D_MODEL))]
    return _token_call(_ffn_kernel, "ffn", FFN_TILE, hp, hs, specs,
                       (gains, w["ffn1_in"], w["ffn1_out"]))


def _ffn_ple(hp, hs, pp, ps, layer, gains, w):
    tile = FFN_PLE_TILE
    specs = [_prompt_rows(PLE_DIM, tile, layer), _sample_rows(PLE_DIM, layer),
             _layer((N_NORMS, D_MODEL), layer), _whole((D_MODEL, 2 * D_FF)),
             _whole((D_FF, D_MODEL)), _whole((D_MODEL, D_MODEL)), _whole((PLE_DIM, D_MODEL))]
    return _token_call(_ffn_ple_kernel, "ffn_ple", tile, hp, hs, specs,
                       (pp, ps, gains, w["ffn2_in"], w["ffn2_out"], w["ple_gate"], w["ple"]))


def _short_conv(cin, cw_ref, prev2, prev1, rows):
    sh1 = jnp.where(rows < 1, prev1, pltpu.roll(cin, 1, 0))
    sh2 = jnp.where(rows < 2, prev2, pltpu.roll(cin, 2, 0))
    return sh2 * cw_ref[0:1, :] + sh1 * cw_ref[1:2, :] + cin * cw_ref[2:3, :]


def _merge_out(x, ua, ub, gates, wo_ref, g_ref):
    merged = jax.nn.sigmoid(gates[:, :D_MODEL]) * ua + jax.nn.sigmoid(gates[:, D_MODEL:]) * ub
    mix = _dot(merged.astype(BF16), wo_ref[...])
    return x + _rms(mix, g_ref, 3)


def _mixer_weight_specs(layer):
    return [_layer((N_NORMS, D_MODEL), layer), _whole((D_MODEL, N_IN)),
            _layer((1, D_RET), layer), _whole((D_RET, D_MODEL)),
            _layer((CONV_W, CONV_CH), layer), _whole((CONV_CH, D_MODEL)),
            _whole((D_MODEL, D_MODEL))]


def _mixer_weights(small, w):
    norm_g, ret_gn, conv_w = small
    return (norm_g, w["in"], ret_gn, w["ret_out"], conv_w, w["conv_out"], w["o"])


def _mixp_kernel(x_ref, g_ref, win_ref, gn_ref, wret_ref, cw_ref, wconv_ref, wo_ref,
                 cos_ref, sin_ref, dmask_ref, qdec_ref, kdec_ref, cdec_ref, *rest, n_cast):
    cast_src = rest[:n_cast]
    o_ref, sret_ref, sconv_ref = rest[n_cast:n_cast + 3]
    cast_dst = rest[n_cast + 3:2 * n_cast + 3]
    ya_ref = rest[-1]
    t = pl.program_id(1)
    step = pl.program_id(0) * pl.num_programs(1) + t

    @pl.when(t == 0)
    def _():
        sret_ref[...] = jnp.zeros_like(sret_ref)
        sconv_ref[...] = jnp.zeros_like(sconv_ref)

    for (_, _, _, blocks), src, dst in zip(CAST_PLAN, cast_src, cast_dst):
        if blocks == MIXP_STEPS:
            dst[...] = src[...].astype(BF16)
        else:
            @pl.when(step < blocks)
            def _(src=src, dst=dst):
                dst[...] = src[...].astype(BF16)

    x = x_ref[...]
    xn = _rms(x, g_ref, 2).astype(BF16)
    cos = cos_ref[...]
    sin = sin_ref[...]

    qk = _dot(xn, win_ref[:, OFF_QK:OFF_VG])
    vg = _dot(xn, win_ref[:, OFF_VG:OFF_CONV])
    cv = _dot(xn, win_ref[:, OFF_CONV:OFF_GATE])
    gates = _dot(xn, win_ref[:, OFF_GATE:])

    cin = cv[:, CONV_CH:2 * CONV_CH] * cv[:, 2 * CONV_CH:]
    rows = lax.broadcasted_iota(jnp.int32, (TOKEN_TILE, CONV_CH), 0)
    prev1 = sconv_ref[7:8, :]
    prev2 = jnp.where(rows < 1, sconv_ref[6:7, :], prev1)
    conv = _short_conv(cin, cw_ref, prev2, prev1, rows)
    sconv_ref[...] = cin[TOKEN_TILE - 8:, :]
    ub = _dot((cv[:, :CONV_CH] * conv).astype(BF16), wconv_ref[...])

    for h in range(HEADS):
        hs = slice(h * HEAD_DIM, (h + 1) * HEAD_DIM)
        ks = slice(D_RET + h * HEAD_DIM, D_RET + (h + 1) * HEAD_DIM)
        qh = _rotary(qk[:, hs], cos, sin)
        kh = _rotary(qk[:, ks], cos, sin) * K_SCALE
        state = sret_ref[h]
        for c in range(TOKEN_TILE // RET_CHUNK):
            rs = slice(c * RET_CHUNK, (c + 1) * RET_CHUNK)
            qc = qh[rs].astype(BF16)
            kc = kh[rs]
            vc = vg[rs, hs].astype(BF16)
            scores = _dot_nt(qc, kc.astype(BF16)) * dmask_ref[h]
            o = _dot(scores.astype(BF16), vc) + _dot(qc, state.astype(BF16)) * qdec_ref[h]
            kd = (kc * kdec_ref[h]).astype(BF16)
            state = cdec_ref[h][0:1, :] * state + _dot_tn(kd, vc)
            on = _group_norm(o, gn_ref[:, hs])
            ya_ref[rs, hs] = (_silu(vg[rs, ks]) * on).astype(BF16)
        sret_ref[h] = state
    ua = _dot(ya_ref[...], wret_ref[...])

    for rb in _tail_blocks(TOKEN_TILE):
        o_ref[rb, :] = _merge_out(x[rb], ua[rb], ub[rb], gates[rb], wo_ref, g_ref)


def _mix_prompt(hp, layer, weights, tables, next_f32):
    nt = SEQ // TOKEN_TILE
    row = pl.BlockSpec((TOKEN_TILE, D_MODEL), lambda b, t: (b * nt + t, 0))
    pos = pl.BlockSpec((TOKEN_TILE, HEAD_DIM), lambda b, t: (t, 0))
    cast_in, cast_out, cast_shapes = [], [], []
    for (_, rows, cols, blocks), _ in zip(CAST_PLAN, next_f32):
        last = blocks - 1
        cast_in.append(pl.BlockSpec(
            (None, rows // blocks, cols),
            lambda b, t, last=last: (layer + 1, jnp.minimum(b * nt + t, last), 0)))
        cast_out.append(pl.BlockSpec(
            (rows // blocks, cols), lambda b, t, last=last: (jnp.minimum(b * nt + t, last), 0)))
        cast_shapes.append(jax.ShapeDtypeStruct((rows, cols), BF16))
    return pl.pallas_call(
        functools.partial(_mixp_kernel, n_cast=len(next_f32)),
        grid=(BATCH, nt),
        in_specs=[row] + _mixer_weight_specs(layer) + [
            pos, pos,
            _whole((HEADS, RET_CHUNK, RET_CHUNK)), _whole((HEADS, RET_CHUNK, HEAD_DIM)),
            _whole((HEADS, RET_CHUNK, HEAD_DIM)), _whole((HEADS, 8, HEAD_DIM))] + cast_in,
        out_specs=[row,
                   pl.BlockSpec((None, HEADS, HEAD_DIM, HEAD_DIM), lambda b, t: (b, 0, 0, 0)),
                   pl.BlockSpec((None, 8, CONV_CH), lambda b, t: (b, 0, 0))] + cast_out,
        out_shape=[jax.ShapeDtypeStruct((N_PROMPT, D_MODEL), F32),
                   jax.ShapeDtypeStruct((BATCH, HEADS, HEAD_DIM, HEAD_DIM), F32),
                   jax.ShapeDtypeStruct((BATCH, 8, CONV_CH), F32)] + cast_shapes,
        scratch_shapes=[pltpu.VMEM((TOKEN_TILE, D_RET), BF16)],
        compiler_params=pltpu.CompilerParams(
            dimension_semantics=("arbitrary", "arbitrary"), vmem_limit_bytes=VMEM_LIMIT),
        name="mix_prompt",
    )(hp, *weights, *tables, *next_f32)


def _mixs_kernel(x_ref, g_ref, win_ref, gn_ref, wret_ref, cw_ref, wconv_ref, wo_ref,
                 cos_ref, sin_ref, bdm_ref, qdec_ref, kdec_ref, cdec_ref, scx_ref, st_ref,
                 *rest):
    (o_ref, sret_ref, ctail_ref,
     q_ref, kd_ref, v_ref, oa_ref, g_scr, ub_ref, gates_ref) = rest[-10:]
    i = pl.program_id(0)
    group_rows = SAMPLE_GROUP * DEC_SEQ

    @pl.when(i == 0)
    def _prologue():
        x = x_ref[...]
        xn = _rms(x, g_ref, 2).astype(BF16)
        cos = cos_ref[...]
        sin = sin_ref[...]
        qk = _dot(xn, win_ref[:, OFF_QK:OFF_VG])
        vg = _dot(xn, win_ref[:, OFF_VG:OFF_CONV])
        v_ref[...] = vg[:, :D_RET]
        g_scr[...] = vg[:, D_RET:]
        for h in range(HEADS):
            hs = slice(h * HEAD_DIM, (h + 1) * HEAD_DIM)
            ks = slice(D_RET + h * HEAD_DIM, D_RET + (h + 1) * HEAD_DIM)
            qh = _rotary(qk[:, hs], cos, sin)
            kh = _rotary(qk[:, ks], cos, sin) * K_SCALE
            q_ref[:, hs] = qh
            for c in range(N_SAMPLE // SAMPLE_BLOCK):
                rs = slice(c * SAMPLE_BLOCK, (c + 1) * SAMPLE_BLOCK)
                kc = kh[rs]
                scores = _dot_nt(qh[rs].astype(BF16), kc.astype(BF16)) * bdm_ref[h]
                oa_ref[rs, hs] = _dot(scores.astype(BF16), vg[rs, hs].astype(BF16))
                kd_ref[rs, hs] = kc * kdec_ref[h]
        cv = _dot(xn, win_ref[:, OFF_CONV:OFF_GATE])
        cin = cv[:, CONV_CH:2 * CONV_CH] * cv[:, 2 * CONV_CH:]
        ctail_ref[...] = cin
        scx = scx_ref[...]
        tpos = lax.broadcasted_iota(jnp.int32, (N_SAMPLE, CONV_CH), 0) & (DEC_SEQ - 1)
        prev1 = pltpu.roll(scx, N_SAMPLE - 1, 0)
        conv = _short_conv(cin, cw_ref, scx, prev1, tpos)
        ub_ref[...] = _dot((cv[:, :CONV_CH] * conv).astype(BF16), wconv_ref[...])
        gates_ref[...] = _dot(xn, win_ref[:, OFF_GATE:])

    r0 = pl.multiple_of(i * group_rows, group_rows)
    first = lax.broadcasted_iota(jnp.int32, (8, HEAD_DIM), 0) < DEC_SEQ
    for h in range(HEADS):
        hs = slice(h * HEAD_DIM, (h + 1) * HEAD_DIM)
        cdec = cdec_ref[h][0:1, :]
        for j in range(SAMPLE_GROUP // 2):
            rt = pl.ds(pl.multiple_of(r0 + j * 8, 8), 8)
            qt = q_ref[rt, hs].astype(BF16)
            kt = kd_ref[rt, hs]
            vt = v_ref[rt, hs].astype(BF16)
            sa = st_ref[2 * j, h]
            sb = st_ref[2 * j + 1, h]
            cross = jnp.where(first, _dot(qt, sa.astype(BF16)), _dot(qt, sb.astype(BF16)))
            oa_ref[rt, hs] = oa_ref[rt, hs] + cross * qdec_ref[h]
            ka = jnp.where(first, kt, 0.0).astype(BF16)
            kb = jnp.where(first, 0.0, kt).astype(BF16)
            sret_ref[2 * j, h] = cdec * sa + _dot_tn(ka, vt)
            sret_ref[2 * j + 1, h] = cdec * sb + _dot_tn(kb, vt)

    @pl.when(i == pl.num_programs(0) - 1)
    def _epilogue():
        for h in range(HEADS):
            hs = slice(h * HEAD_DIM, (h + 1) * HEAD_DIM)
            on = _group_norm(oa_ref[:, hs], gn_ref[:, hs])
            g_scr[:, hs] = _silu(g_scr[:, hs]) * on
        ua = _dot(g_scr[...].astype(BF16), wret_ref[...])
        for rb in _tail_blocks(N_SAMPLE):
            o_ref[rb, :] = _merge_out(x_ref[rb, :], ua[rb], ub_ref[rb, :], gates_ref[rb, :],
                                      wo_ref, g_ref)


def _mix_sample(hs, layer, weights, tables, scx, state_ret, ret_acc):
    row = pl.BlockSpec((N_SAMPLE, D_MODEL), lambda i: (0, 0))
    st_block = (None, SAMPLE_GROUP, HEADS, HEAD_DIM, HEAD_DIM)
    st_spec = pl.BlockSpec(st_block, lambda i: (layer, i, 0, 0, 0))
    in_specs = [row] + _mixer_weight_specs(layer) + [
        _whole((N_SAMPLE, HEAD_DIM)), _whole((N_SAMPLE, HEAD_DIM)),
        _whole((HEADS, SAMPLE_BLOCK, SAMPLE_BLOCK)), _whole((HEADS, 8, HEAD_DIM)),
        _whole((HEADS, SAMPLE_BLOCK, HEAD_DIM)), _whole((HEADS, 8, HEAD_DIM)),
        _layer((N_SAMPLE, CONV_CH), layer), st_spec]
    in_specs.append(pl.BlockSpec(memory_space=pl.ANY))
    args = [hs, *weights, *tables, scx, state_ret, ret_acc]
    aliases = {len(args) - 1: 1}
    return pl.pallas_call(
        _mixs_kernel,
        grid=(DEC_BATCH // SAMPLE_GROUP,),
        in_specs=in_specs,
        out_specs=[row, st_spec, pl.BlockSpec((N_SAMPLE, CONV_CH), lambda i: (0, 0))],
        out_shape=[jax.ShapeDtypeStruct((N_SAMPLE, D_MODEL), F32),
                   jax.ShapeDtypeStruct((DEPTH, DEC_BATCH, HEADS, HEAD_DIM, HEAD_DIM), F32),
                   jax.ShapeDtypeStruct((N_SAMPLE, CONV_CH), F32)],
        scratch_shapes=[pltpu.VMEM((N_SAMPLE, D_RET), F32),
                        pltpu.VMEM((N_SAMPLE, D_RET), F32),
                        pltpu.VMEM((N_SAMPLE, D_RET), F32),
                        pltpu.VMEM((N_SAMPLE, D_RET), F32),
                        pltpu.VMEM((N_SAMPLE, D_RET), F32),
                        pltpu.VMEM((N_SAMPLE, D_MODEL), F32),
                        pltpu.VMEM((N_SAMPLE, 2 * D_MODEL), F32)],
        input_output_aliases=aliases,
        compiler_params=pltpu.CompilerParams(
            dimension_semantics=("arbitrary",), vmem_limit_bytes=VMEM_LIMIT),
        name="mix_sample",
    )(*args)


def _rope_tables(pos):
    inv_freq = ROPE_BASE ** (-jnp.arange(0, HEAD_DIM, 2, dtype=F32) / HEAD_DIM)
    ang = pos[:, None] * inv_freq[None, :]
    cos = jnp.cos(ang)
    sin = jnp.sin(ang)
    return jnp.concatenate([cos, cos], -1), jnp.concatenate([-sin, sin], -1)


def _decay_tables(chunk):
    log_gamma = jnp.log1p(-jnp.exp2(-5.0 - jnp.arange(HEADS, dtype=F32)))
    idx = jnp.arange(chunk, dtype=F32)
    diff = idx[:, None] - idx[None, :]
    dmask = jnp.where(diff[None] >= 0,
                      jnp.exp(log_gamma[:, None, None] * jnp.maximum(diff, 0.0)[None]), 0.0)
    qdec = jnp.exp(log_gamma[:, None] * (idx[None] + 1.0))
    kdec = jnp.exp(log_gamma[:, None] * (chunk - 1.0 - idx[None]))
    cdec = jnp.exp(log_gamma * chunk)
    return dmask, qdec, kdec, cdec


def _lanes(t, reps):
    t = jnp.tile(t, (1, reps))
    return jnp.broadcast_to(t[:, :, None], t.shape + (HEAD_DIM,))


def _prompt_tables():
    cos, sin = _rope_tables(jnp.arange(SEQ, dtype=F32))
    dmask, qdec, kdec, cdec = _decay_tables(RET_CHUNK)
    cdec = jnp.broadcast_to(cdec[:, None, None], (HEADS, 8, HEAD_DIM))
    return cos, sin, dmask, _lanes(qdec, 1), _lanes(kdec, 1), cdec


def _sample_tables():
    cos, sin = _rope_tables(PAST_LEN + jnp.arange(DEC_SEQ, dtype=F32))
    cos = jnp.tile(cos, (DEC_BATCH, 1))
    sin = jnp.tile(sin, (DEC_BATCH, 1))
    dmask, qdec, kdec, cdec = _decay_tables(DEC_SEQ)
    seqs_per_block = SAMPLE_BLOCK // DEC_SEQ
    eye = jnp.eye(seqs_per_block, dtype=F32)
    bdm = jnp.einsum("ab,hij->haibj", eye, dmask).reshape(HEADS, SAMPLE_BLOCK, SAMPLE_BLOCK)
    cdec = jnp.broadcast_to(cdec[:, None, None], (HEADS, 8, HEAD_DIM))
    return cos, sin, bdm, _lanes(qdec, 8 // DEC_SEQ), _lanes(kdec, seqs_per_block), cdec


def kernel(x_prompt, x_sample, state_ret, state_conv, p_prompt, p_sample, norm_g, w_ffn1_in,
           w_ffn1_out, w_in, ret_gn, w_ret_out, conv_w, w_conv_out, w_o, w_ffn2_in, w_ffn2_out,
           w_ple_gate, w_ple):
    hp = x_prompt.reshape(N_PROMPT, D_MODEL)
    hs = x_sample.reshape(N_SAMPLE, D_MODEL)
    pp = p_prompt.reshape(DEPTH, N_PROMPT, PLE_DIM)
    ps = p_sample.reshape(DEPTH, N_SAMPLE, PLE_DIM)
    tables_p = _prompt_tables()
    tables_s = _sample_tables()

    f32_weights = (w_ffn1_in, w_ffn1_out, w_in, w_ret_out, w_conv_out, w_o, w_ffn2_in,
                   w_ffn2_out, w_ple_gate, w_ple)
    names = [name for name, _, _, _ in CAST_PLAN]
    w = {name: wf[0].astype(BF16) for name, wf in zip(names, f32_weights)}
    small = (norm_g, ret_gn.reshape(DEPTH, 1, D_RET), conv_w)
    scx = jnp.concatenate([state_conv, jnp.zeros_like(state_conv)], axis=2)
    scx = scx.reshape(DEPTH, N_SAMPLE, CONV_CH)

    ret_p, conv_p, conv_s = [], [], []
    ret_s = jnp.zeros((DEPTH, DEC_BATCH, HEADS, HEAD_DIM, HEAD_DIM), F32)
    for i in range(DEPTH):
        hp, hs = _ffn(hp, hs, i, norm_g, w)
        mixer_w = _mixer_weights(small, w)
        hp, rp, cp, *w_next = _mix_prompt(hp, i, mixer_w, tables_p,
                                          f32_weights if i + 1 < DEPTH else ())
        hs, ret_s, ct = _mix_sample(hs, i, mixer_w, tables_s, scx, state_ret, ret_s)
        hp, hs = _ffn_ple(hp, hs, pp, ps, i, norm_g, w)
        if w_next:
            w = dict(zip(names, w_next))
        ret_p.append(rp)
        conv_p.append(cp[:, 8 - (CONV_W - 1):, :])
        conv_s.append(ct.reshape(DEC_BATCH, DEC_SEQ, CONV_CH)[:, DEC_SEQ - (CONV_W - 1):, :])

    return (hp.reshape(BATCH, SEQ, D_MODEL), hs.reshape(DEC_BATCH, DEC_SEQ, D_MODEL),
            jnp.stack(ret_p), jnp.stack(conv_p), ret_s, jnp.stack(conv_s))
```
